```python
import math
import jax
import jax.numpy as jnp
from jax import lax
import numpy as np

D_MODEL = 4096
BATCH = 8
SEQ = 2048
DEPTH = 4
DEC_BATCH = 32
DEC_SEQ = 64
PAST_LEN = 1024

CHUNK = 64
N_META = 16
EPS = 1e-6
A_HD = 64
A_WIDTH = D_MODEL // 2
A_HEADS = A_WIDTH // (2 * A_HD)
A_QK = 2 * A_HEADS * A_HD
ROT_DIM = A_HD // 4
ROPE_THETA = 500000.0
Q_BLOCK = 128
S_INNER = D_MODEL // 2
S_HD = 64
S_HEADS = S_INNER // S_HD
S_GROUPS = 4
S_N = 128
S_CONV = 4
S_CONV_DIM = S_INNER + 2 * S_GROUPS * S_N
SSD_BLOCK = 64
G_DK = 256
G_DV = 512
G_HEADS = D_MODEL // G_DV
G_RANK = 16
GATE_TAU = 16.0
GLA_BLOCK = 16
D_FF = ((8 * D_MODEL // 3 + 255) // 256) * 256
F_CONV = 3
N_EVEN = (DEPTH + 1) // 2
N_ODD = DEPTH // 2
EVEN_SPLITS = (A_QK, A_QK, A_WIDTH, S_INNER, S_CONV_DIM, S_HEADS)
EVEN_IN = 2 * A_QK + A_WIDTH + S_INNER + S_CONV_DIM + S_HEADS
ODD_SPLITS = (G_HEADS * G_DK, G_HEADS * G_DK, G_HEADS * G_DV, G_HEADS * G_DV, G_RANK)
ODD_IN = 2 * G_HEADS * G_DK + 2 * G_HEADS * G_DV + G_RANK

kernel_name = 'hybrid_streaming_encoder_step'


def rmsnorm(x, g):
    xf = x.astype(jnp.float32)
    y = xf * lax.rsqrt(jnp.mean(xf * xf, axis=-1, keepdims=True) + EPS)
    return (y * g.astype(jnp.float32)).astype(x.dtype)


def split_cols(t, sizes):
    return jnp.split(t, np.cumsum(sizes)[:-1].tolist(), axis=-1)


def pad_len(t, mult):
    l = t.shape[1]
    lp = -(-l // mult) * mult
    return jnp.pad(t, [(0, 0), (0, lp - l)] + [(0, 0)] * (t.ndim - 2))


def blocks(t, size):
    b, l = t.shape[:2]
    return t.reshape((b, l // size, size) + t.shape[2:]).swapaxes(0, 1)


def unblocks(t, l):
    nb, b, size = t.shape[:3]
    return t.swapaxes(0, 1).reshape((b, nb * size) + t.shape[3:])[:, :l]


def partial_rope(t, pos):
    half = ROT_DIM // 2
    inv = jnp.exp(-math.log(ROPE_THETA) * jnp.arange(half, dtype=jnp.float32) * 2.0 / ROT_DIM)
    ang = pos.astype(jnp.float32)[:, None] * inv[None, :]
    cos = jnp.cos(ang)[None, :, None, :]
    sin = jnp.sin(ang)[None, :, None, :]
    t1 = t[..., :half]
    t2 = t[..., half:ROT_DIM]
    return jnp.concatenate([t1 * cos - t2 * sin, t2 * cos + t1 * sin, t[..., ROT_DIM:]], axis=-1).astype(t.dtype)


def causal_dwconv(u, w, bias, prev):
    ext = jnp.concatenate([prev.astype(u.dtype), u], axis=1)
    out = lax.conv_general_dilated(ext, w[:, None, :], window_strides=(1,), padding='VALID',
                                   dimension_numbers=('NWC', 'WIO', 'NWC'),
                                   feature_group_count=u.shape[-1])
    return out + bias, ext[:, ext.shape[1] - (w.shape[0] - 1):]


def diff_attend(q, k, v, lam, mask):
    b, nq = q.shape[:2]
    s = jnp.einsum('bqhd,bkhd->bhqk', q, k).astype(jnp.float32) * (A_HD ** -0.5)
    if mask is not None:
        s = jnp.where(mask, s, -jnp.inf)
    p = jax.nn.softmax(s, axis=-1).reshape(b, A_HEADS, 2, nq, k.shape[1])
    attn = (p[:, :, 0] - lam.astype(jnp.float32) * p[:, :, 1]).astype(v.dtype)
    return jnp.einsum('bhqk,bkhe->bqhe', attn, v)


def diff_attn_blocked(q, k, v, lam, chunk_ids):
    l = q.shape[1]
    qb = blocks(pad_len(q, Q_BLOCK), Q_BLOCK)
    nb = qb.shape[0]
    cq = jnp.pad(chunk_ids, (0, nb * Q_BLOCK - l), constant_values=l).reshape(nb, Q_BLOCK)

    def one(args):
        q_blk, c_blk = args
        return diff_attend(q_blk, k, v, lam, c_blk[:, None] >= chunk_ids[None, :])

    return unblocks(lax.map(one, (qb, cq)), l)


def ssd_scan(x, dt, a, bm, cm, state0):
    b, l, h, p = x.shape
    r = h // S_GROUPS
    xs = blocks(pad_len(x.reshape(b, l, S_GROUPS, r, p), SSD_BLOCK), SSD_BLOCK)
    dts = blocks(pad_len(dt.reshape(b, l, S_GROUPS, r), SSD_BLOCK), SSD_BLOCK)
    bs = blocks(pad_len(bm, SSD_BLOCK), SSD_BLOCK)
    cs = blocks(pad_len(cm, SSD_BLOCK), SSD_BLOCK)
    a = a.reshape(S_GROUPS, r)
    causal = jnp.tril(jnp.ones((SSD_BLOCK, SSD_BLOCK), bool))[None, :, :, None, None]

    def body(st, inp):
        xb, dtb, bb, cb = inp
        cum = jnp.cumsum(dtb * a, axis=1)
        seg = cum[:, :, None] - cum[:, None, :]
        decay = jnp.exp(jnp.where(causal, seg, -jnp.inf))
        w = jnp.einsum('btgn,bsgn->btsg', cb, bb)[..., None] * decay
        xdt = xb * dtb[..., None]
        y = (jnp.einsum('btsgr,bsgrp->btgrp', w, xdt)
             + jnp.einsum('btgn,bgrpn->btgrp', cb, st) * jnp.exp(cum)[..., None])
        tail = jnp.exp(cum[:, -1:] - cum)
        st = (st * jnp.exp(cum[:, -1])[..., None, None]
              + jnp.einsum('bsgn,bsgrp->bgrpn', bb, xdt * tail[..., None]))
        return st, y

    st, ys = lax.scan(body, state0.reshape(b, S_GROUPS, r, p, S_N), (xs, dts, bs, cs))
    return unblocks(ys, l).reshape(b, l, h, p), st.reshape(b, h, p, S_N)


def gla_scan(q, k, v, lg, state0):
    l = q.shape[1]
    qs, kk, vs, gs = [blocks(pad_len(t, GLA_BLOCK), GLA_BLOCK) for t in (q, k, v, lg)]
    causal = jnp.tril(jnp.ones((GLA_BLOCK, GLA_BLOCK), bool))

    def body(s, inp):
        qb, kb, vb, gb = inp
        cum = jnp.cumsum(gb, axis=1)
        q_in = qb * jnp.exp(cum)
        k_in = kb * jnp.exp(-cum)
        att = jnp.where(causal, jnp.einsum('bthd,bshd->bhts', q_in, k_in), 0)
        o = jnp.einsum('bhts,bshe->bthe', att, vb) + jnp.einsum('bthd,bhde->bthe', q_in, s)
        k_tail = kb * jnp.exp(cum[:, -1:] - cum)
        s = s * jnp.exp(cum[:, -1])[..., None] + jnp.einsum('bshd,bshe->bhde', k_tail, vb)
        return s, o

    s, os_ = lax.scan(body, state0, (qs, kk, vs, gs))
    return unblocks(os_, l), s


def even_mixer(h, pos, chunk_ids, k_past, v_past, ssm0, conv0, lambda_init,
               w_in, lq1, lk1, lq2, lk2, subln, conv_w, conv_b, dt_bias, a_log, d_skip, s_norm, w_out):
    b, l, _ = h.shape
    q, k, v, z, xbc, dt = split_cols(h @ w_in, EVEN_SPLITS)
    q = partial_rope(q.reshape(b, l, 2 * A_HEADS, A_HD), pos)
    k = partial_rope(k.reshape(b, l, 2 * A_HEADS, A_HD), pos)
    v = v.reshape(b, l, A_HEADS, 2 * A_HD)
    lam = jnp.exp(jnp.sum(lq1 * lk1)) - jnp.exp(jnp.sum(lq2 * lk2)) + lambda_init
    if k_past is None:
        o_a = diff_attn_blocked(q, k, v, lam, chunk_ids)
    else:
        o_a = diff_attend(q, jnp.concatenate([k_past, k], axis=1),
                          jnp.concatenate([v_past, v], axis=1), lam, None)
    o_a = rmsnorm(o_a, subln) * (1.0 - lambda_init)
    xbc, conv_new = causal_dwconv(xbc, conv_w, conv_b, conv0)
    xbc = jax.nn.silu(xbc)
    xs, bm, cm = split_cols(xbc, (S_INNER, S_GROUPS * S_N, S_GROUPS * S_N))
    xs = xs.reshape(b, l, S_HEADS, S_HD)
    dt = jax.nn.softplus(dt + dt_bias)
    y, ssm_new = ssd_scan(xs, dt, -jnp.exp(a_log), bm.reshape(b, l, S_GROUPS, S_N),
                          cm.reshape(b, l, S_GROUPS, S_N), ssm0)
    y = y + xs * d_skip[:, None]
    y = (y.reshape(b, l, S_INNER) * jax.nn.silu(z)).reshape(b, l, S_GROUPS, S_INNER // S_GROUPS)
    y = rmsnorm(y, s_norm.reshape(S_GROUPS, S_INNER // S_GROUPS))
    mix = jnp.concatenate([o_a.reshape(b, l, A_WIDTH), y.reshape(b, l, S_INNER)], axis=-1)
    return mix @ w_out, k, v, ssm_new, conv_new


def odd_mixer(h, gla0, w_in, w_gate2, b_gate, g_norm, w_out):
    b, l, _ = h.shape
    q, k, v, r, g_low = split_cols(h @ w_in, ODD_SPLITS)
    lg = jax.nn.log_sigmoid(g_low @ w_gate2 + b_gate) / GATE_TAU
    q = q.reshape(b, l, G_HEADS, G_DK) * (G_DK ** -0.5)
    o, s_new = gla_scan(q, k.reshape(b, l, G_HEADS, G_DK), v.reshape(b, l, G_HEADS, G_DV),
                        lg.reshape(b, l, G_HEADS, G_DK), gla0)
    o = rmsnorm(o, g_norm).reshape(b, l, G_HEADS * G_DV) * jax.nn.silu(r)
    return o @ w_out, s_new


def conv_ffn(h, w_up, conv_w, conv_b, w_down, prev):
    gate, val = jnp.split(h @ w_up, 2, axis=-1)
    gate, new_prev = causal_dwconv(gate, conv_w, conv_b, prev)
    return (jax.nn.gelu(gate) * val) @ w_down, new_prev


def run_trunk(h, pos, chunk_ids, k_past, v_past, ssm0, sconv0, gla0, fconv0, p):
    ks, vs, ssms, sconvs, glas, fconvs = [], [], [], [], [], []
    for i in range(DEPTH):
        hn = rmsnorm(h, p['norm_pre_mix'][i])
        if i % 2 == 0:
            e = i // 2
            out, k_new, v_new, ssm_new, sconv_new = even_mixer(
                hn, pos, chunk_ids,
                None if k_past is None else k_past[e],
                None if v_past is None else v_past[e],
                ssm0[e], sconv0[e], 0.8 - 0.6 * math.exp(-0.3 * i),
                p['w_in_even'][e], p['lambda_q1'][e], p['lambda_k1'][e], p['lambda_q2'][e],
                p['lambda_k2'][e], p['attn_subln'][e], p['ssm_conv_w'][e], p['ssm_conv_b'][e],
                p['ssm_dt_bias'][e], p['ssm_a_log'][e], p['ssm_d'][e], p['ssm_norm'][e],
                p['w_out_even'][e])
            ks.append(k_new)
            vs.append(v_new)
            ssms.append(ssm_new)
            sconvs.append(sconv_new)
        else:
            o = i // 2
            out, gla_new = odd_mixer(hn, gla0[o], p['w_in_odd'][o], p['gla_w_gate2'][o],
                                     p['gla_b_gate'][o], p['gla_norm'][o], p['w_out_odd'][o])
            glas.append(gla_new)
        h = h + rmsnorm(out, p['norm_post_mix'][i])
        f, fconv_new = conv_ffn(rmsnorm(h, p['norm_pre_ffn'][i]), p['w_up'][i], p['ffn_conv_w'][i],
                                p['ffn_conv_b'][i], p['w_down'][i], fconv0[i])
        fconvs.append(fconv_new)
        h = h + rmsnorm(f, p['norm_post_ffn'][i])
    return (h, jnp.stack(ks), jnp.stack(vs), jnp.stack(ssms), jnp.stack(sconvs),
            jnp.stack(glas), jnp.stack(fconvs))


def setup_inputs(seed: int = 0) -> dict:
    key = jax.random.key(seed)
    ks = iter(jax.random.split(key, 48))
    f32 = jnp.float32

    def nrm(shape, scale):
        return jax.random.normal(next(ks), shape, f32) * scale

    def gain(shape):
        return 1.0 + nrm(shape, 0.05)

    u = jax.random.uniform(next(ks), (N_EVEN, S_HEADS), f32)
    dt0 = jnp.exp(u * (math.log(0.1) - math.log(0.001)) + math.log(0.001))
    return {
        'x_prompt': nrm((BATCH, SEQ, D_MODEL), 1.0),
        'x_sample': nrm((DEC_BATCH, DEC_SEQ, D_MODEL), 1.0),
        'cache_k': nrm((N_EVEN, DEC_BATCH, PAST_LEN, 2 * A_HEADS, A_HD), 1.0),
        'cache_v': nrm((N_EVEN, DEC_BATCH, PAST_LEN, A_HEADS, 2 * A_HD), 1.0),
        'state_ssm': nrm((N_EVEN, DEC_BATCH, S_HEADS, S_HD, S_N), 0.5),
        'state_ssm_conv': nrm((N_EVEN, DEC_BATCH, S_CONV - 1, S_CONV_DIM), 1.0),
        'state_gla': nrm((N_ODD, DEC_BATCH, G_HEADS, G_DK, G_DV), 1.0),
        'state_ffn_conv': nrm((DEPTH, DEC_BATCH, F_CONV - 1, D_FF), 1.0),
        'meta_tokens': nrm((N_META, D_MODEL), 1.0),
        'norm_pre_mix': gain((DEPTH, D_MODEL)),
        'norm_post_mix': gain((DEPTH, D_MODEL)),
        'norm_pre_ffn': gain((DEPTH, D_MODEL)),
        'norm_post_ffn': gain((DEPTH, D_MODEL)),
        'w_in_even': nrm((N_EVEN, D_MODEL, EVEN_IN), D_MODEL ** -0.5),
        'lambda_q1': nrm((N_EVEN, A_HD), 0.1),
        'lambda_k1': nrm((N_EVEN, A_HD), 0.1),
        'lambda_q2': nrm((N_EVEN, A_HD), 0.1),
        'lambda_k2': nrm((N_EVEN, A_HD), 0.1),
        'attn_subln': gain((N_EVEN, 2 * A_HD)),
        'ssm_conv_w': nrm((N_EVEN, S_CONV, S_CONV_DIM), S_CONV ** -0.5),
        'ssm_conv_b': nrm((N_EVEN, S_CONV_DIM), 0.02),
        'ssm_dt_bias': dt0 + jnp.log(-jnp.expm1(-dt0)),
        'ssm_a_log': jnp.log(jax.random.uniform(next(ks), (N_EVEN, S_HEADS), f32, 1.0, 16.0)),
        'ssm_d': gain((N_EVEN, S_HEADS)),
        'ssm_norm': gain((N_EVEN, S_INNER)),
        'w_out_even': nrm((N_EVEN, A_WIDTH + S_INNER, D_MODEL), (A_WIDTH + S_INNER) ** -0.5),
        'w_in_odd': nrm((N_ODD, D_MODEL, ODD_IN), D_MODEL ** -0.5),
        'gla_w_gate2': nrm((N_ODD, G_RANK, G_HEADS * G_DK), G_RANK ** -0.5),
        'gla_b_gate': nrm((N_ODD, G_HEADS * G_DK), 0.02),
        'gla_norm': gain((N_ODD, G_DV)),
        'w_out_odd': nrm((N_ODD, G_HEADS * G_DV, D_MODEL), (G_HEADS * G_DV) ** -0.5),
        'w_up': nrm((DEPTH, D_MODEL, 2 * D_FF), D_MODEL ** -0.5),
        'ffn_conv_w': nrm((DEPTH, F_CONV, D_FF), F_CONV ** -0.5),
        'ffn_conv_b': nrm((DEPTH, D_FF), 0.02),
        'w_down': nrm((DEPTH, D_FF, D_MODEL), D_FF ** -0.5),
    }


def reference(x_prompt, x_sample, cache_k, cache_v, state_ssm, state_ssm_conv, state_gla,
              state_ffn_conv, meta_tokens, norm_pre_mix, norm_post_mix, norm_pre_ffn, norm_post_ffn,
              w_in_even, lambda_q1, lambda_k1, lambda_q2, lambda_k2, attn_subln, ssm_conv_w,
              ssm_conv_b, ssm_dt_bias, ssm_a_log, ssm_d, ssm_norm, w_out_even, w_in_odd,
              gla_w_gate2, gla_b_gate, gla_norm, w_out_odd, w_up, ffn_conv_w, ffn_conv_b, w_down):
    p = dict(norm_pre_mix=norm_pre_mix, norm_post_mix=norm_post_mix, norm_pre_ffn=norm_pre_ffn,
             norm_post_ffn=norm_post_ffn, w_in_even=w_in_even, lambda_q1=lambda_q1,
             lambda_k1=lambda_k1, lambda_q2=lambda_q2, lambda_k2=lambda_k2, attn_subln=attn_subln,
             ssm_conv_w=ssm_conv_w, ssm_conv_b=ssm_conv_b, ssm_dt_bias=ssm_dt_bias,
             ssm_a_log=ssm_a_log, ssm_d=ssm_d, ssm_norm=ssm_norm, w_out_even=w_out_even,
             w_in_odd=w_in_odd, gla_w_gate2=gla_w_gate2, gla_b_gate=gla_b_gate, gla_norm=gla_norm,
             w_out_odd=w_out_odd, w_up=w_up, ffn_conv_w=ffn_conv_w, ffn_conv_b=ffn_conv_b,
             w_down=w_down)
    b, seq = x_prompt.shape[:2]
    dtp = x_prompt.dtype
    h0 = jnp.concatenate([jnp.broadcast_to(meta_tokens.astype(dtp)[None], (b, N_META, D_MODEL)),
                          x_prompt], axis=1)
    pos_p = jnp.arange(N_META + seq)
    chunk_ids = jnp.concatenate([jnp.zeros((N_META,), jnp.int32),
                                 jnp.arange(seq, dtype=jnp.int32) // CHUNK + 1])
    hp, p_k, p_v, p_ssm, p_ssm_conv, p_gla, p_ffn_conv = run_trunk(
        h0, pos_p, chunk_ids, None, None,
        jnp.zeros((N_EVEN, b, S_HEADS, S_HD, S_N), dtp),
        jnp.zeros((N_EVEN, b, S_CONV - 1, S_CONV_DIM), dtp),
        jnp.zeros((N_ODD, b, G_HEADS, G_DK, G_DV), dtp),
        jnp.zeros((DEPTH, b, F_CONV - 1, D_FF), dtp), p)
    y_prompt = hp[:, N_META:]
    pos_s = cache_k.shape[2] + jnp.arange(x_sample.shape[1])
    y_sample, s_k, s_v, s_ssm, s_ssm_conv, s_gla, s_ffn_conv = run_trunk(
        x_sample, pos_s, None, cache_k, cache_v, state_ssm, state_ssm_conv, state_gla,
        state_ffn_conv, p)
    return (y_prompt, y_sample, p_k, p_v, p_ssm, p_ssm_conv, p_gla, p_ffn_conv,
            s_k, s_v, s_ssm, s_ssm_conv, s_gla, s_ffn_conv)
```

```python
import functools
import math

import jax
import jax.numpy as jnp
from jax import lax
from jax.experimental import pallas as pl
from jax.experimental.pallas import tpu as pltpu

F32 = jnp.float32
BF16 = jnp.bfloat16

CHUNK = 64
N_META = 16
EPS = 1e-6
A_HD = 64
ROT_DIM = A_HD // 4
ROPE_THETA = 500000.0
S_HD = 64
S_GROUPS = 4
S_N = 128
S_CONV = 4
G_DK = 256
G_DV = 512
G_RANK = 16
GATE_TAU = 16.0
F_CONV = 3

LANE = 128
SUBLANE = 8
BF16_ROWS = 16
V7X_VMEM_BYTES = 64 * 1024 * 1024
VMEM_CAP = V7X_VMEM_BYTES - 8 * 1024 * 1024

FF_TILE = 512
SSD_CHUNK = 128
GLA_CHUNK = 32


def _params(semantics, vmem_estimate):
    limit = int(min(VMEM_CAP, max(32 * 1024 * 1024, vmem_estimate * 5 // 4)))
    return pltpu.CompilerParams(dimension_semantics=semantics, vmem_limit_bytes=limit)


def _divisor_tile(n, limit, mult):
    best = None
    for t in range(mult, min(n, limit) + 1, mult):
        if n % t == 0:
            best = t
    assert best is not None, (n, limit, mult)
    return best


def _rms(x, g):
    return x * lax.rsqrt(jnp.mean(x * x, axis=-1, keepdims=True) + EPS) * g


def _split3(x):
    hi = x.astype(BF16)
    r1 = x - hi.astype(F32)
    mid = r1.astype(BF16)
    lo = (r1 - mid.astype(F32)).astype(BF16)
    return hi, mid, lo


def _dot(a, b):
    return jnp.dot(a, b, preferred_element_type=F32)


def _dot_nt(a, b):
    return lax.dot_general(a, b, (((1,), (1,)), ((), ())), preferred_element_type=F32)


def _dot_tn(a, b):
    return lax.dot_general(a, b, (((0,), (0,)), ((), ())), preferred_element_type=F32)


def _prenorm_kernel(h_ref, g_ref, o_ref):
    o_ref[...] = _rms(h_ref[...], g_ref[...]).astype(o_ref.dtype)


def prenorm(h, g):
    rows, d = h.shape
    tr = _divisor_tile(rows, 256, BF16_ROWS)
    est = 2 * tr * d * (4 + 2)
    return pl.pallas_call(
        _prenorm_kernel,
        out_shape=jax.ShapeDtypeStruct((rows, d), BF16),
        grid=(rows // tr,),
        in_specs=[pl.BlockSpec((tr, d), lambda i: (i, 0)), pl.BlockSpec((1, d), lambda i: (0, 0))],
        out_specs=pl.BlockSpec((tr, d), lambda i: (i, 0)),
        compiler_params=_params(("parallel",), est),
        name="prenorm",
    )(h, g.reshape(1, d))


def _resnorm_kernel(h_ref, u_ref, gpost_ref, gpre_ref, hnew_ref, hn_ref):
    hnew = h_ref[...] + _rms(u_ref[...], gpost_ref[...])
    hnew_ref[...] = hnew
    hn_ref[...] = _rms(hnew, gpre_ref[...]).astype(hn_ref.dtype)


def _resnorm_last_kernel(h_ref, u_ref, gpost_ref, hnew_ref):
    hnew_ref[...] = h_ref[...] + _rms(u_ref[...], gpost_ref[...])


def residual_norm(h, u, g_post, g_pre):
    rows, d = h.shape
    tr = _divisor_tile(rows, 256, BF16_ROWS)
    row_spec = pl.BlockSpec((tr, d), lambda i: (i, 0))
    g_spec = pl.BlockSpec((1, d), lambda i: (0, 0))
    if g_pre is None:
        return pl.pallas_call(
            _resnorm_last_kernel,
            out_shape=jax.ShapeDtypeStruct((rows, d), F32),
            grid=(rows // tr,),
            in_specs=[row_spec, row_spec, g_spec],
            out_specs=row_spec,
            compiler_params=_params(("parallel",), 2 * tr * d * 12),
            name="resnorm_last",
        )(h, u, g_post.reshape(1, d)), None
    return pl.pallas_call(
        _resnorm_kernel,
        out_shape=(jax.ShapeDtypeStruct((rows, d), F32), jax.ShapeDtypeStruct((rows, d), BF16)),
        grid=(rows // tr,),
        in_specs=[row_spec, row_spec, g_spec, g_spec],
        out_specs=(row_spec, row_spec),
        compiler_params=_params(("parallel",), 2 * tr * d * 14),
        name="resnorm",
    )(h, u, g_post.reshape(1, d), g_pre.reshape(1, d))


def _mm_kernel(*refs, nx):
    o_ref = refs[2 * nx]
    acc = _dot(refs[0][...], refs[nx][...])
    for t in range(1, nx):
        acc = acc + _dot(refs[t][...], refs[nx + t][...])
    o_ref[...] = acc.astype(o_ref.dtype)


def matmul(xs, ws, out_dtype=F32):
    m = xs[0].shape[0]
    n = ws[0].shape[1]
    bm = _divisor_tile(m, 1024, BF16_ROWS)
    bn = _divisor_tile(n, 1024, LANE)
    ktot = sum(x.shape[1] for x in xs)
    est = 2 * (bm * ktot * 2 + ktot * bn * 2 + bm * bn * 4)
    in_specs = ([pl.BlockSpec((bm, x.shape[1]), lambda j, i: (i, 0)) for x in xs]
                + [pl.BlockSpec((w.shape[0], bn), lambda j, i: (0, j)) for w in ws])
    return pl.pallas_call(
        functools.partial(_mm_kernel, nx=len(xs)),
        out_shape=jax.ShapeDtypeStruct((m, n), out_dtype),
        grid=(n // bn, m // bm),
        in_specs=in_specs,
        out_specs=pl.BlockSpec((bm, bn), lambda j, i: (i, j)),
        compiler_params=_params(("parallel", "parallel"), est),
        name="matmul",
    )(*xs, *ws)


def _mmk_kernel(x_ref, w_ref, o_ref, acc_ref):
    k = pl.program_id(2)
    part = _dot(x_ref[...], w_ref[...])

    @pl.when(k == 0)
    def _():
        acc_ref[...] = part

    @pl.when(k > 0)
    def _():
        acc_ref[...] += part

    @pl.when(k == pl.num_programs(2) - 1)
    def _():
        o_ref[...] = acc_ref[...]


def matmul_ktiled(x, w, bk):
    m, kdim = x.shape
    n = w.shape[1]
    bm = _divisor_tile(m, 1408, BF16_ROWS)
    bn = _divisor_tile(n, 1024, LANE)
    est = 2 * (bm * bk * 2 + bk * bn * 2 + bm * bn * 4) + bm * bn * 4
    return pl.pallas_call(
        _mmk_kernel,
        out_shape=jax.ShapeDtypeStruct((m, n), F32),
        grid=(n // bn, m // bm, kdim // bk),
        in_specs=[pl.BlockSpec((bm, bk), lambda j, i, k: (i, k)),
                  pl.BlockSpec((bk, bn), lambda j, i, k: (k, j))],
        out_specs=pl.BlockSpec((bm, bn), lambda j, i, k: (i, j)),
        scratch_shapes=[pltpu.VMEM((bm, bn), F32)],
        compiler_params=_params(("parallel", "parallel", "arbitrary"), est),
        name="matmul_ktiled",
    )(x, w)


def _ffn_up_kernel(x_ref, wg_ref, wv_ref, cw_ref, cb_ref, prev_ref, a_ref, st_ref, carry_ref,
                   *, nseq, rows, nblk):
    x = x_ref[...]
    g = _dot(x, wg_ref[...])
    v = _dot(x, wv_ref[...])
    tf = g.shape[1]
    if nblk == 1:
        prev = prev_ref[...]
    else:
        first = (pl.program_id(1) % nblk) == 0

        @pl.when(first)
        def _():
            carry_ref[...] = jnp.zeros(carry_ref.shape, F32)

        prev = jnp.where(first, prev_ref[...], carry_ref[SUBLANE - 2:SUBLANE, :][None])
        carry_ref[...] = g[rows - SUBLANE:rows, :]
    g3 = g.reshape(nseq, rows, tf)
    g1 = pltpu.roll(g, 1, axis=0).reshape(nseq, rows, tf)
    g2 = pltpu.roll(g, 2, axis=0).reshape(nseq, rows, tf)
    t = lax.broadcasted_iota(jnp.int32, (1, rows, 1), 1)
    p0 = prev[:, 0:1, :]
    p1 = prev[:, 1:2, :]
    g1 = jnp.where(t == 0, p1, g1)
    g2 = jnp.where(t == 0, p0, jnp.where(t == 1, p1, g2))
    cw = cw_ref[...]
    c = cw[0:1, :] * g2 + cw[1:2, :] * g1 + cw[2:3, :] * g3 + cb_ref[...]
    a = jax.nn.gelu(c) * v.reshape(nseq, rows, tf)
    a_ref[...] = a.reshape(nseq * rows, tf).astype(a_ref.dtype)
    st_ref[...] = g3[:, rows - 2:rows, :]


def ffn_up(hn, w_up, conv_w, conv_b, prev, seq_len):
    m, d = hn.shape
    f = w_up.shape[1] // 2
    nb = m // seq_len
    tf = FF_TILE
    nt = f // tf
    if seq_len >= 512:
        rows = _divisor_tile(seq_len, 1024, BF16_ROWS)
        nseq, nblk = 1, seq_len // rows
    else:
        rows, nblk = seq_len, 1
        nseq = _divisor_tile(nb, max(1, 1024 // seq_len), 1)
    bm = nseq * rows
    est = 2 * (bm * d * 2 + 2 * d * tf * 2 + bm * tf * 2) + 6 * bm * tf * 4
    a, st = pl.pallas_call(
        functools.partial(_ffn_up_kernel, nseq=nseq, rows=rows, nblk=nblk),
        out_shape=(jax.ShapeDtypeStruct((m, f), BF16), jax.ShapeDtypeStruct((nb, F_CONV - 1, f), F32)),
        grid=(nt, m // bm),
        in_specs=[pl.BlockSpec((bm, d), lambda j, i: (i, 0)),
                  pl.BlockSpec((d, tf), lambda j, i: (0, j)),
                  pl.BlockSpec((d, tf), lambda j, i: (0, nt + j)),
                  pl.BlockSpec((F_CONV, tf), lambda j, i: (0, j)),
                  pl.BlockSpec((1, tf), lambda j, i: (0, j)),
                  pl.BlockSpec((nseq, F_CONV - 1, tf), lambda j, i: (i // nblk, 0, j))],
        out_specs=(pl.BlockSpec((bm, tf), lambda j, i: (i, j)),
                   pl.BlockSpec((nseq, F_CONV - 1, tf), lambda j, i: (i // nblk, 0, j))),
        scratch_shapes=[pltpu.VMEM((SUBLANE, tf), F32)],
        compiler_params=_params(("parallel", "arbitrary"), est),
        name="ffn_up",
    )(hn, w_up, w_up, conv_w, conv_b.reshape(1, f), prev)
    return a, st


def _rope_tables(pos):
    half = ROT_DIM // 2
    inv = jnp.exp(-math.log(ROPE_THETA) * jnp.arange(half, dtype=F32) * 2.0 / ROT_DIM)
    ang = pos.astype(F32)[:, None] * inv[None, :]
    cos, sin = jnp.cos(ang), jnp.sin(ang)
    n = pos.shape[0]
    one = jnp.ones((n, A_HD - ROT_DIM), F32)
    zero = jnp.zeros((n, A_HD - ROT_DIM), F32)
    zh = jnp.zeros((n, half), F32)
    c = jnp.concatenate([cos, cos, one], axis=1)
    s_up = jnp.concatenate([-sin, zh, zero], axis=1)
    s_dn = jnp.concatenate([zh, sin, zero], axis=1)
    return tuple(jnp.concatenate([t, t], axis=1) for t in (c, s_up, s_dn))


def _rope(x, c, s_up, s_dn):
    half = ROT_DIM // 2
    return (x * c + pltpu.roll(x, 2 * A_HD - half, axis=1) * s_up + pltpu.roll(x, half, axis=1) * s_dn)


def _lambda(lq1, lk1, lq2, lk2, lambda_init):
    l1 = jnp.sum(lq1[...] * lk1[...], axis=-1, keepdims=True)
    l2 = jnp.sum(lq2[...] * lk2[...], axis=-1, keepdims=True)
    return jnp.exp(l1) - jnp.exp(l2) + lambda_init


def _two_softmax_pv(q, parts, lam):
    lane = lax.broadcasted_iota(jnp.int32, q.shape, 1)
    outs = []
    for sel in (lane < A_HD, lane >= A_HD):
        qs = jnp.where(sel, q, 0.0).astype(BF16)
        scores = []
        for kb, _, mask in parts:
            s = _dot_nt(qs, kb)
            if mask is not None:
                s = jnp.where(mask, s, -jnp.inf)
            scores.append(s)
        mx = scores[0].max(axis=-1, keepdims=True)
        for s in scores[1:]:
            mx = jnp.maximum(mx, s.max(axis=-1, keepdims=True))
        den = None
        num = None
        for s, (_, vb, _) in zip(scores, parts):
            p = jnp.exp(s - mx)
            d = jnp.sum(p, axis=-1, keepdims=True)
            o = _dot(p.astype(BF16), vb)
            den = d if den is None else den + d
            num = o if num is None else num + o
        outs.append(num / den)
    return outs[0] - lam * outs[1]


def _attn_prompt_kernel(q_ref, k_ref, v_ref, c_ref, su_ref, sd_ref, lq1, lk1, lq2, lk2, sub_ref,
                        o_ref, kr_ref, kb_ref, vb_ref, *, seq_len, pad_len, lambda_init):
    lam = _lambda(lq1, lk1, lq2, lk2, lambda_init)
    c, su, sd = c_ref[...], su_ref[...], sd_ref[...]
    k = _rope(k_ref[...], c, su, sd)
    kr_ref[...] = k
    kb_ref[0:seq_len, :] = k.astype(BF16)
    vb_ref[0:seq_len, :] = v_ref[...].astype(BF16)
    if pad_len > seq_len:
        kb_ref[seq_len:pad_len, :] = jnp.zeros((pad_len - seq_len, 2 * A_HD), BF16)
        vb_ref[seq_len:pad_len, :] = jnp.zeros((pad_len - seq_len, 2 * A_HD), BF16)
    qb = LANE
    shift = CHUNK - N_META
    for r0 in range(0, seq_len, qb):
        r1 = min(seq_len, r0 + qb)
        last_chunk = (r1 - 1 + shift) // CHUNK
        ke = min(pad_len, -(-(N_META + CHUNK * last_chunk) // LANE) * LANE)
        q = _rope(q_ref[r0:r1, :], c[r0:r1], su[r0:r1], sd[r0:r1]) * (A_HD ** -0.5)
        cq = (lax.broadcasted_iota(jnp.int32, (r1 - r0, 1), 0) + (r0 + shift)) // CHUNK
        ck = (lax.broadcasted_iota(jnp.int32, (1, ke), 1) + shift) // CHUNK
        o = _two_softmax_pv(q, [(kb_ref[0:ke, :], vb_ref[0:ke, :], cq >= ck)], lam)
        o_ref[r0:r1, :] = (_rms(o, sub_ref[...]) * (1.0 - lambda_init)).astype(o_ref.dtype)


def _attn_sample_kernel(q_ref, k_ref, v_ref, kp_ref, vp_ref, c_ref, su_ref, sd_ref, lq1, lk1, lq2, lk2,
                        sub_ref, o_ref, kr_ref, *, lambda_init):
    lam = _lambda(lq1, lk1, lq2, lk2, lambda_init)
    c, su, sd = c_ref[...], su_ref[...], sd_ref[...]
    k = _rope(k_ref[...], c, su, sd)
    kr_ref[...] = k
    q = _rope(q_ref[...], c, su, sd) * (A_HD ** -0.5)
    parts = [(kp_ref[0].astype(BF16), vp_ref[0].astype(BF16), None),
             (k.astype(BF16), v_ref[...].astype(BF16), None)]
    o = _two_softmax_pv(q, parts, lam)
    o_ref[...] = (_rms(o, sub_ref[...]) * (1.0 - lambda_init)).astype(o_ref.dtype)


def diff_attention(proj, pos, seq_len, n_heads, lams, subln, lambda_init, k_past=None, v_past=None):
    m = proj.shape[0]
    nb = m // seq_len
    hw = 2 * A_HD
    tabs = _rope_tables(pos)
    vec = lambda: pl.BlockSpec((1, A_HD), lambda b, h: (0, 0))
    tab = lambda: pl.BlockSpec((seq_len, hw), lambda b, h: (0, 0))
    col = lambda off: pl.BlockSpec((seq_len, hw), lambda b, h: (b, off + h))
    out_specs = (pl.BlockSpec((seq_len, hw), lambda b, h: (b, h)),
                 pl.BlockSpec((seq_len, hw), lambda b, h: (b, h)))
    out_shape = (jax.ShapeDtypeStruct((m, n_heads * hw), BF16), jax.ShapeDtypeStruct((m, n_heads * hw), F32))
    lam_args = [l.reshape(1, A_HD) for l in lams]
    sub = subln.reshape(1, hw)
    sub_spec = pl.BlockSpec((1, hw), lambda b, h: (0, 0))
    if k_past is None:
        pad_len = -(-seq_len // LANE) * LANE
        est = 2 * seq_len * hw * (3 * 4 + 3 * 4 + 2 + 4) + 2 * pad_len * hw * 2 + 8 * LANE * pad_len * 4
        return pl.pallas_call(
            functools.partial(_attn_prompt_kernel, seq_len=seq_len, pad_len=pad_len, lambda_init=lambda_init),
            out_shape=out_shape,
            grid=(nb, n_heads),
            in_specs=[col(0), col(n_heads), col(2 * n_heads), tab(), tab(), tab(),
                      vec(), vec(), vec(), vec(), sub_spec],
            out_specs=out_specs,
            scratch_shapes=[pltpu.VMEM((pad_len, hw), BF16), pltpu.VMEM((pad_len, hw), BF16)],
            compiler_params=_params(("parallel", "parallel"), est),
            name="attn_prompt",
        )(proj, proj, proj, *tabs, *lam_args, sub)
    past = k_past.shape[1]
    kp = k_past.reshape(nb, past, n_heads * hw)
    vp = v_past.reshape(nb, past, n_heads * hw)
    past_spec = lambda: pl.BlockSpec((1, past, hw), lambda b, h: (b, 0, h))
    est = 2 * (2 * past * hw * 4 + 8 * seq_len * hw * 4) + 8 * seq_len * past * 4
    return pl.pallas_call(
        functools.partial(_attn_sample_kernel, lambda_init=lambda_init),
        out_shape=out_shape,
        grid=(nb, n_heads),
        in_specs=[col(0), col(n_heads), col(2 * n_heads), past_spec(), past_spec(), tab(), tab(), tab(),
                  vec(), vec(), vec(), vec(), sub_spec],
        out_specs=out_specs,
        compiler_params=_params(("parallel", "parallel"), est),
        name="attn_sample",
    )(proj, proj, proj, kp, vp, *tabs, *lam_args, sub)


def _cumsum_rows(x):
    n = x.shape[0]
    row = lax.broadcasted_iota(jnp.int32, (n, 1), 0)
    sh = 1
    while sh < n:
        x = x + jnp.where(row >= sh, pltpu.roll(x, sh, axis=0), 0.0)
        sh *= 2
    return x


def _conv4_silu(ext, w, b, rows):
    acc = b + w[S_CONV - 1:S_CONV, :] * ext[SUBLANE:SUBLANE + rows, :]
    for j in range(S_CONV - 1):
        back = S_CONV - 1 - j
        acc = acc + w[j:j + 1, :] * pltpu.roll(ext, back, axis=0)[SUBLANE:SUBLANE + rows, :]
    return jax.nn.silu(acc)


def _ssd_kernel(xs_ref, bm_ref, cm_ref, z_ref, dt_ref, cwx_ref, cwb_ref, cwc_ref, cbx_ref, cbb_ref, cbc_ref,
                c0x_ref, c0b_ref, c0c_ref, dtb_ref, alog_ref, dsk_ref, nrm_ref, st0_ref,
                y_ref, stout_ref, s_ref, *, seq_len, chunk, has_state):
    hp = xs_ref.shape[1]
    nh = hp // S_HD
    if has_state:
        s_ref[...] = st0_ref[0].reshape(hp, S_N).T
    else:
        s_ref[...] = jnp.zeros((S_N, hp), F32)
    a_neg = -jnp.exp(alog_ref[...])
    dtb = dtb_ref[...]
    e_row = lax.broadcasted_iota(jnp.int32, (LANE, hp), 0)
    e_col = lax.broadcasted_iota(jnp.int32, (LANE, hp), 1) // S_HD
    expand = (e_row == e_col).astype(BF16)
    eye = (lax.broadcasted_iota(jnp.int32, (LANE, LANE), 0)
           == lax.broadcasted_iota(jnp.int32, (LANE, LANE), 1)).astype(BF16)
    lane_hp = lax.broadcasted_iota(jnp.int32, (1, 2 * S_HD), 1)

    def pad8(c0_ref):
        c0 = c0_ref[0]
        return jnp.concatenate([jnp.zeros((SUBLANE - (S_CONV - 1), c0.shape[1]), F32), c0], axis=0)

    def ext_of(ref, start, rows, c0_ref):
        if isinstance(start, int) and start == 0:
            return jnp.concatenate([pad8(c0_ref), ref[0:rows, :]], axis=0)
        return ref[pl.ds(start - SUBLANE, rows + SUBLANE), :]

    def to_heads(v):
        hi, mid, lo = _split3(v)
        return _dot(hi, expand) + _dot(mid, expand) + _dot(lo, expand)

    def step(start, rows):
        x = _conv4_silu(ext_of(xs_ref, start, rows, c0x_ref), cwx_ref[...], cbx_ref[...], rows)
        bmat = _conv4_silu(ext_of(bm_ref, start, rows, c0b_ref), cwb_ref[...], cbb_ref[...], rows)
        cmat = _conv4_silu(ext_of(cm_ref, start, rows, c0c_ref), cwc_ref[...], cbc_ref[...], rows)
        dt = jax.nn.softplus(dt_ref[pl.ds(start, rows), :] + dtb)
        cum = _cumsum_rows(dt * a_neg)
        hi, mid, lo = _split3(cum)
        cum_t = _dot_nt(eye, hi) + _dot_nt(eye, mid) + _dot_nt(eye, lo)
        cumx = to_heads(cum)
        xdt = x * to_heads(dt)
        xdt_b = xdt.astype(BF16)
        c_b = cmat.astype(BF16)
        b_b = bmat.astype(BF16)
        cb = _dot_nt(c_b, b_b)
        causal = (lax.broadcasted_iota(jnp.int32, (rows, rows), 0)
                  >= lax.broadcasted_iota(jnp.int32, (rows, rows), 1))
        cols = []
        for pair in range(nh // 2):
            xpair = xdt_b[:, pair * 2 * S_HD:(pair + 1) * 2 * S_HD]
            acc = None
            for sub in range(2):
                h = 2 * pair + sub
                seg = cum[:, h:h + 1] - cum_t[h:h + 1, :]
                wgt = (cb * jnp.exp(jnp.where(causal, seg, -jnp.inf))).astype(BF16)
                keep = (lane_hp < S_HD) if sub == 0 else (lane_hp >= S_HD)
                part = _dot(wgt, jnp.where(keep, xpair, jnp.zeros_like(xpair)))
                acc = part if acc is None else acc + part
            cols.append(acc)
        y = jnp.concatenate(cols, axis=1)
        s_old = s_ref[...]
        y = y + _dot(c_b, s_old.astype(BF16)) * jnp.exp(cumx)
        y = y + x * dsk_ref[...]
        last = cumx[rows - 1:rows, :]
        tail = jnp.exp(last - cumx)
        s_ref[...] = s_old * jnp.exp(last) + _dot_tn(b_b, (xdt * tail).astype(BF16))
        yz = y * jax.nn.silu(z_ref[pl.ds(start, rows), :])
        y_ref[pl.ds(start, rows), :] = _rms(yz, nrm_ref[...]).astype(y_ref.dtype)

    head = seq_len % chunk
    if head:
        step(0, head)
    n_full = seq_len // chunk
    if n_full == 1 and head == 0:
        step(0, chunk)
    elif n_full:
        def body(i, carry):
            step(pl.multiple_of(head + i * chunk, BF16_ROWS), chunk)
            return carry
        if head == 0:
            step(0, chunk)
            lax.fori_loop(1, n_full, body, 0)
        else:
            lax.fori_loop(0, n_full, body, 0)
    stout_ref[0] = s_ref[...].T.reshape(nh, S_HD, S_N)


def ssd_mixer(proj, dtp, x_off, z_off, seq_len, conv_w, conv_b, conv0, dt_bias, a_log, d_skip, s_norm, state0):
    m = proj.shape[0]
    nb = m // seq_len
    heads = a_log.shape[0]
    inner = heads * S_HD
    g = (conv_w.shape[1] - inner) // (2 * S_N)
    nh = heads // g
    hp = nh * S_HD
    chunk = min(SSD_CHUNK, seq_len)

    def per_head(v):
        return jnp.zeros((g, LANE), F32).at[:, :nh].set(v.reshape(g, nh)).reshape(1, g * LANE)

    xb, bb, cb = x_off // hp, (x_off + inner) // S_N, (x_off + inner + g * S_N) // S_N
    row = lambda width, blk: pl.BlockSpec((seq_len, width), lambda b, j: (b, blk + j))
    wrow = lambda rows, width, blk: pl.BlockSpec((rows, width), lambda b, j: (0, blk + j))
    c0 = lambda width, blk: pl.BlockSpec((1, S_CONV - 1, width), lambda b, j: (b, 0, blk + j))
    has_state = state0 is not None
    if not has_state:
        state0 = jnp.zeros((1, heads, S_HD, S_N), F32)
        st_spec = pl.BlockSpec((1, nh, S_HD, S_N), lambda b, j: (0, j, 0, 0))
    else:
        st_spec = pl.BlockSpec((1, nh, S_HD, S_N), lambda b, j: (b, j, 0, 0))
    cbias = conv_b.reshape(1, -1)
    ib, icb, icc = 0, inner // S_N, (inner + g * S_N) // S_N
    est = 2 * seq_len * (2 * hp * 4 + 3 * LANE * 4 + hp * 2) + 40 * chunk * hp * 4
    return pl.pallas_call(
        functools.partial(_ssd_kernel, seq_len=seq_len, chunk=chunk, has_state=has_state),
        out_shape=(jax.ShapeDtypeStruct((m, inner), BF16), jax.ShapeDtypeStruct((nb, heads, S_HD, S_N), F32)),
        grid=(nb, g),
        in_specs=[row(hp, xb), row(S_N, bb), row(S_N, cb), row(hp, z_off // hp),
                  pl.BlockSpec((seq_len, LANE), lambda b, j: (b, j)),
                  wrow(S_CONV, hp, ib), wrow(S_CONV, S_N, icb), wrow(S_CONV, S_N, icc),
                  wrow(1, hp, ib), wrow(1, S_N, icb), wrow(1, S_N, icc),
                  c0(hp, ib), c0(S_N, icb), c0(S_N, icc),
                  wrow(1, LANE, 0), wrow(1, LANE, 0), wrow(1, hp, 0), wrow(1, hp, 0), st_spec],
        out_specs=(pl.BlockSpec((seq_len, hp), lambda b, j: (b, j)),
                   pl.BlockSpec((1, nh, S_HD, S_N), lambda b, j: (b, j, 0, 0))),
        scratch_shapes=[pltpu.VMEM((S_N, hp), F32)],
        compiler_params=_params(("parallel", "parallel"), est),
        name="ssd",
    )(proj, proj, proj, proj, dtp, conv_w, conv_w, conv_w, cbias, cbias, cbias, conv0, conv0, conv0,
      per_head(dt_bias), per_head(a_log), jnp.repeat(d_skip, S_HD).reshape(1, inner), s_norm.reshape(1, inner),
      state0)


def _gla_kernel(q_ref, k_ref, v_ref, r_ref, gl_ref, w2_ref, bg_ref, gn_ref, s0_ref, o_ref, sout_ref, s_ref,
                *, seq_len, chunk, has_state):
    if has_state:
        s_ref[...] = s0_ref[0, 0].T
    else:
        s_ref[...] = jnp.zeros(s_ref.shape, F32)
    scale = G_DK ** -0.5

    def step(start, rows):
        sl = pl.ds(start, rows)
        zg = _dot(gl_ref[sl, :].astype(BF16), w2_ref[...]) + bg_ref[...]
        cum = _cumsum_rows(jax.nn.log_sigmoid(zg) / GATE_TAU)
        q = q_ref[sl, :] * scale
        k = k_ref[sl, :]
        v_b = v_ref[sl, :].astype(BF16)
        anchor = cum[rows // 2 - 1:rows // 2, :] if rows > 1 else cum
        qa = (q * jnp.exp(cum - anchor)).astype(BF16)
        ka = (k * jnp.exp(anchor - cum)).astype(BF16)
        causal = (lax.broadcasted_iota(jnp.int32, (rows, rows), 0)
                  >= lax.broadcasted_iota(jnp.int32, (rows, rows), 1))
        att = jnp.where(causal, _dot_nt(qa, ka), 0.0).astype(BF16)
        s_old = s_ref[...]
        o = _dot(att, v_b) + _dot_nt((q * jnp.exp(cum)).astype(BF16), s_old.astype(BF16))
        last = cum[rows - 1:rows, :]
        k_tail = (k * jnp.exp(last - cum)).astype(BF16)
        s_ref[...] = s_old * jnp.exp(last) + _dot_tn(v_b, k_tail)
        o_ref[sl, :] = (_rms(o, gn_ref[...]) * jax.nn.silu(r_ref[sl, :])).astype(o_ref.dtype)

    head = seq_len % chunk
    if head:
        step(0, head)
    n_full = seq_len // chunk

    def body(i, carry):
        step(pl.multiple_of(head + i * chunk, BF16_ROWS), chunk)
        return carry
    lax.fori_loop(0, n_full, body, 0)
    sout_ref[0, 0] = s_ref[...].T


def gla_mixer(proj, glow, seq_len, n_heads, w_gate2, b_gate, g_norm, state0):
    m = proj.shape[0]
    nb = m // seq_len
    kq, vq = n_heads * G_DK, n_heads * G_DV
    has_state = state0 is not None
    if not has_state:
        state0 = jnp.zeros((1, 1, G_DK, G_DV), F32)
        st_spec = pl.BlockSpec((1, 1, G_DK, G_DV), lambda b, h: (0, 0, 0, 0))
    else:
        st_spec = pl.BlockSpec((1, 1, G_DK, G_DV), lambda b, h: (b, h, 0, 0))
    w2 = jnp.zeros((LANE, kq), BF16).at[:G_RANK].set(w_gate2.astype(BF16))
    row = lambda width, blk: pl.BlockSpec((seq_len, width), lambda b, h: (b, blk + h))
    est = 2 * seq_len * (2 * G_DK * 4 + 2 * G_DV * 4 + LANE * 4 + G_DV * 2) + 6 * G_DK * G_DV * 4
    return pl.pallas_call(
        functools.partial(_gla_kernel, seq_len=seq_len, chunk=GLA_CHUNK, has_state=has_state),
        out_shape=(jax.ShapeDtypeStruct((m, vq), BF16), jax.ShapeDtypeStruct((nb, n_heads, G_DK, G_DV), F32)),
        grid=(nb, n_heads),
        in_specs=[row(G_DK, 0), row(G_DK, kq // G_DK), row(G_DV, 2 * kq // G_DV), row(G_DV, (2 * kq + vq) // G_DV),
                  pl.BlockSpec((seq_len, LANE), lambda b, h: (b, 0)),
                  pl.BlockSpec((LANE, G_DK), lambda b, h: (0, h)),
                  pl.BlockSpec((1, G_DK), lambda b, h: (0, h)),
                  pl.BlockSpec((1, G_DV), lambda b, h: (0, 0)),
                  st_spec],
        out_specs=(pl.BlockSpec((seq_len, G_DV), lambda b, h: (b, h)),
                   pl.BlockSpec((1, 1, G_DK, G_DV), lambda b, h: (b, h, 0, 0))),
        scratch_shapes=[pltpu.VMEM((G_DV, G_DK), F32)],
        compiler_params=_params(("parallel", "parallel"), est),
        name="gla",
    )(proj, proj, proj, proj, glow, w2, b_gate.reshape(1, kq), g_norm.reshape(1, G_DV), state0)


def _prepare_weights(p, d_model):
    depth = p["w_up"].shape[0]
    n_even = p["w_in_even"].shape[0]
    heads_s = p["ssm_a_log"].shape[1]
    inner = heads_s * S_HD
    a_width = d_model // 2
    a_qk = a_width
    conv_dim = p["ssm_conv_w"].shape[2]
    groups = (conv_dim - inner) // (2 * S_N)
    main_e = 2 * a_qk + a_width + inner + conv_dim
    nh = heads_s // groups
    f = p["w_up"].shape[2] // 2
    fp = -(-f // FF_TILE) * FF_TILE
    kq = p["gla_w_gate2"].shape[2]
    main_o = 2 * kq + 2 * d_model
    w = {}
    w["in_even"] = [p["w_in_even"][e, :, :main_e].astype(BF16) for e in range(n_even)]
    dtw = p["w_in_even"][:, :, main_e:].reshape(n_even, d_model, groups, nh)
    dtw = jnp.pad(dtw, ((0, 0), (0, 0), (0, 0), (0, LANE - nh))).reshape(n_even, d_model, groups * LANE)
    w["dt"] = [dtw[e].astype(BF16) for e in range(n_even)]
    w["out_even_a"] = [p["w_out_even"][e, :a_width].astype(BF16) for e in range(n_even)]
    w["out_even_s"] = [p["w_out_even"][e, a_width:].astype(BF16) for e in range(n_even)]
    n_odd = p["w_in_odd"].shape[0]
    w["in_odd"] = [p["w_in_odd"][o, :, :main_o].astype(BF16) for o in range(n_odd)]
    w["glow"] = [jnp.pad(p["w_in_odd"][o, :, main_o:], ((0, 0), (0, LANE - G_RANK))).astype(BF16) for o in range(n_odd)]
    w["out_odd"] = [p["w_out_odd"][o].astype(BF16) for o in range(n_odd)]
    padf = ((0, 0), (0, fp - f))
    w["up"] = [jnp.concatenate([jnp.pad(p["w_up"][i, :, :f], padf), jnp.pad(p["w_up"][i, :, f:], padf)],
                               axis=1).astype(BF16) for i in range(depth)]
    w["down"] = [jnp.pad(p["w_down"][i], ((0, fp - f), (0, 0))).astype(BF16) for i in range(depth)]
    w["ffn_conv_w"] = [jnp.pad(p["ffn_conv_w"][i], padf) for i in range(depth)]
    w["ffn_conv_b"] = [jnp.pad(p["ffn_conv_b"][i], (0, fp - f)) for i in range(depth)]
    w["f"], w["fp"], w["main_e"] = f, fp, main_e
    return w


def _run_trunk(h, seq_len, pos, k_past, v_past, ssm0, sconv0, gla0, fconv0, p, w):
    rows, d = h.shape
    nb = rows // seq_len
    depth = p["w_up"].shape[0]
    a_heads = d // 2 // (2 * A_HD)
    a_w = a_heads * 2 * A_HD
    heads_s = p["ssm_a_log"].shape[1]
    inner = heads_s * S_HD
    conv_dim = p["ssm_conv_w"].shape[2]
    g_heads = d // G_DV
    f, fp = w["f"], w["fp"]
    ks, vs, ssms, sconvs, glas, fconvs = [], [], [], [], [], []
    hn = prenorm(h, p["norm_pre_mix"][0])
    for i in range(depth):
        if i % 2 == 0:
            e = i // 2
            lambda_init = 0.8 - 0.6 * math.exp(-0.3 * i)
            proj = matmul([hn], [w["in_even"][e]])
            dtp = matmul([hn], [w["dt"][e]])
            lams = (p["lambda_q1"][e], p["lambda_k1"][e], p["lambda_q2"][e], p["lambda_k2"][e])
            o_a, k_rot = diff_attention(proj, pos, seq_len, a_heads, lams, p["attn_subln"][e], lambda_init,
                                        None if k_past is None else k_past[e],
                                        None if v_past is None else v_past[e])
            conv0 = jnp.zeros((nb, S_CONV - 1, conv_dim), F32) if sconv0 is None else sconv0[e]
            y_s, ssm_new = ssd_mixer(proj, dtp, 3 * a_w + inner, 3 * a_w, seq_len, p["ssm_conv_w"][e],
                                     p["ssm_conv_b"][e], conv0, p["ssm_dt_bias"][e], p["ssm_a_log"][e],
                                     p["ssm_d"][e], p["ssm_norm"][e], None if ssm0 is None else ssm0[e])
            out = matmul([o_a, y_s], [w["out_even_a"][e], w["out_even_s"][e]])
            proj3 = proj.reshape(nb, seq_len, -1)
            ks.append(k_rot.reshape(nb, seq_len, 2 * a_heads, A_HD))
            vs.append(proj3[:, :, 2 * a_w:3 * a_w].reshape(nb, seq_len, a_heads, 2 * A_HD))
            ssms.append(ssm_new)
            sconvs.append(proj3[:, seq_len - (S_CONV - 1):, 3 * a_w + inner:3 * a_w + inner + conv_dim])
        else:
            o = i // 2
            proj = matmul([hn], [w["in_odd"][o]])
            glow = matmul([hn], [w["glow"][o]])
            y_g, gla_new = gla_mixer(proj, glow, seq_len, g_heads, p["gla_w_gate2"][o], p["gla_b_gate"][o],
                                     p["gla_norm"][o], None if gla0 is None else gla0[o])
            out = matmul([y_g], [w["out_odd"][o]])
            glas.append(gla_new)
        h, hn = residual_norm(h, out, p["norm_post_mix"][i], p["norm_pre_ffn"][i])
        prev = jnp.zeros((nb, F_CONV - 1, fp), F32) if fconv0 is None else jnp.pad(fconv0[i], ((0, 0), (0, 0), (0, fp - f)))
        a, fconv_new = ffn_up(hn, w["up"][i], w["ffn_conv_w"][i], w["ffn_conv_b"][i], prev, seq_len)
        fconvs.append(fconv_new[:, :, :f])
        ff = matmul_ktiled(a, w["down"][i], fp // 8 if (fp // 8) % LANE == 0 else fp)
        h, hn = residual_norm(h, ff, p["norm_post_ffn"][i], p["norm_pre_mix"][i + 1] if i + 1 < depth else None)
    return (h, jnp.stack(ks), jnp.stack(vs), jnp.stack(ssms), jnp.stack(sconvs), jnp.stack(glas), jnp.stack(fconvs))


def kernel(x_prompt, x_sample, cache_k, cache_v, state_ssm, state_ssm_conv, state_gla, state_ffn_conv, meta_tokens, norm_pre_mix, norm_post_mix, norm_pre_ffn, norm_post_ffn, w_in_even, lambda_q1, lambda_k1, lambda_q2, lambda_k2, attn_subln, ssm_conv_w, ssm_conv_b, ssm_dt_bias, ssm_a_log, ssm_d, ssm_norm, w_out_even, w_in_odd, gla_w_gate2, gla_b_gate, gla_norm, w_out_odd, w_up, ffn_conv_w, ffn_conv_b, w_down):
    p = dict(norm_pre_mix=norm_pre_mix, norm_post_mix=norm_post_mix, norm_pre_ffn=norm_pre_ffn,
             norm_post_ffn=norm_post_ffn, w_in_even=w_in_even, lambda_q1=lambda_q1, lambda_k1=lambda_k1,
             lambda_q2=lambda_q2, lambda_k2=lambda_k2, attn_subln=attn_subln, ssm_conv_w=ssm_conv_w,
             ssm_conv_b=ssm_conv_b, ssm_dt_bias=ssm_dt_bias, ssm_a_log=ssm_a_log, ssm_d=ssm_d,
             ssm_norm=ssm_norm, w_out_even=w_out_even, w_in_odd=w_in_odd, gla_w_gate2=gla_w_gate2,
             gla_b_gate=gla_b_gate, gla_norm=gla_norm, w_out_odd=w_out_odd, w_up=w_up,
             ffn_conv_w=ffn_conv_w, ffn_conv_b=ffn_conv_b, w_down=w_down)
    b, seq, d = x_prompt.shape
    w = _prepare_weights(p, d)
    n_meta = meta_tokens.shape[0]
    lp = n_meta + seq
    h0 = jnp.concatenate([jnp.broadcast_to(meta_tokens[None], (b, n_meta, d)), x_prompt], axis=1).reshape(b * lp, d)
    hp, p_k, p_v, p_ssm, p_ssm_conv, p_gla, p_ffn_conv = _run_trunk(
        h0, lp, jnp.arange(lp), None, None, None, None, None, None, p, w)
    y_prompt = hp.reshape(b, lp, d)[:, n_meta:]
    bs, ls, _ = x_sample.shape
    pos_s = cache_k.shape[2] + jnp.arange(ls)
    ys, s_k, s_v, s_ssm, s_ssm_conv, s_gla, s_ffn_conv = _run_trunk(
        x_sample.reshape(bs * ls, d), ls, pos_s, cache_k, cache_v, state_ssm, state_ssm_conv, state_gla,
        state_ffn_conv, p, w)
    return (y_prompt, ys.reshape(bs, ls, d), p_k, p_v, p_ssm, p_ssm_conv, p_gla, p_ffn_conv,
            s_k, s_v, s_ssm, s_ssm_conv, s_gla, s_ffn_conv)
```

```python
import functools
import math

import jax
import jax.numpy as jnp
from jax import lax
from jax.experimental import pallas as pl
from jax.experimental.pallas import tpu as pltpu

F32 = jnp.float32
BF16 = jnp.bfloat16

CHUNK = 64
N_META = 16
EPS = 1e-6
A_HD = 64
ROT_DIM = A_HD // 4
ROPE_THETA = 500000.0
S_HD = 64
S_GROUPS = 4
S_N = 128
S_CONV = 4
G_DK = 256
G_DV = 512
G_RANK = 16
GATE_TAU = 16.0
F_CONV = 3

LANE = 128
SUBLANE = 8
BF16_ROWS = 16
MXU_WIDTH = 256
V7X_VMEM_BYTES = 64 * 1024 * 1024
VMEM_CAP = V7X_VMEM_BYTES - 8 * 1024 * 1024

MM_VMEM_BUDGET = 44 * 1024 * 1024

FF_TILE = 1024
FF_SUB = 256
ATTN_QBLOCK = 512
SSD_CHUNK = 128
GLA_CHUNK = 32
GLA_BLOCK = 256


def _params(semantics, vmem_estimate):
    limit = int(min(VMEM_CAP, max(32 * 1024 * 1024, vmem_estimate * 3 // 2)))
    return pltpu.CompilerParams(dimension_semantics=semantics, vmem_limit_bytes=limit)


def _divisor_tile(n, limit, mult):
    best = None
    for t in range(mult, min(n, limit) + 1, mult):
        if n % t == 0:
            best = t
    assert best is not None, (n, limit, mult)
    return best


def _rms(x, g):
    return x * lax.rsqrt(jnp.mean(x * x, axis=-1, keepdims=True) + EPS) * g


def _split3(x):
    hi = x.astype(BF16)
    r1 = x - hi.astype(F32)
    mid = r1.astype(BF16)
    lo = (r1 - mid.astype(F32)).astype(BF16)
    return hi, mid, lo


def _dot(a, b):
    return jnp.dot(a, b, preferred_element_type=F32)


def _dot_nt(a, b):
    return lax.dot_general(a, b, (((1,), (1,)), ((), ())), preferred_element_type=F32)


def _dot_tn(a, b):
    return lax.dot_general(a, b, (((0,), (0,)), ((), ())), preferred_element_type=F32)


def _prenorm_kernel(h_ref, g_ref, o_ref):
    o_ref[...] = _rms(h_ref[...], g_ref[...]).astype(o_ref.dtype)


def prenorm(h, g):
    rows, d = h.shape
    tr = _divisor_tile(rows, 256, BF16_ROWS)
    est = 2 * tr * d * (4 + 2)
    return pl.pallas_call(
        _prenorm_kernel,
        out_shape=jax.ShapeDtypeStruct((rows, d), BF16),
        grid=(rows // tr,),
        in_specs=[pl.BlockSpec((tr, d), lambda i: (i, 0)), pl.BlockSpec((1, d), lambda i: (0, 0))],
        out_specs=pl.BlockSpec((tr, d), lambda i: (i, 0)),
        compiler_params=_params(("parallel",), est),
        name="prenorm",
    )(h, g.reshape(1, d))


def _resnorm_kernel(h_ref, u_ref, gpost_ref, gpre_ref, hnew_ref, hn_ref):
    hnew = h_ref[...] + _rms(u_ref[...], gpost_ref[...])
    hnew_ref[...] = hnew
    hn_ref[...] = _rms(hnew, gpre_ref[...]).astype(hn_ref.dtype)


def _resnorm_last_kernel(h_ref, u_ref, gpost_ref, hnew_ref):
    hnew_ref[...] = h_ref[...] + _rms(u_ref[...], gpost_ref[...])


def residual_norm(h, u, g_post, g_pre):
    rows, d = h.shape
    tr = _divisor_tile(rows, 256, BF16_ROWS)
    row_spec = pl.BlockSpec((tr, d), lambda i: (i, 0))
    g_spec = pl.BlockSpec((1, d), lambda i: (0, 0))
    if g_pre is None:
        return pl.pallas_call(
            _resnorm_last_kernel,
            out_shape=jax.ShapeDtypeStruct((rows, d), F32),
            grid=(rows // tr,),
            in_specs=[row_spec, row_spec, g_spec],
            out_specs=row_spec,
            compiler_params=_params(("parallel",), 2 * tr * d * 12),
            name="resnorm_last",
        )(h, u, g_post.reshape(1, d)), None
    return pl.pallas_call(
        _resnorm_kernel,
        out_shape=(jax.ShapeDtypeStruct((rows, d), F32), jax.ShapeDtypeStruct((rows, d), BF16)),
        grid=(rows // tr,),
        in_specs=[row_spec, row_spec, g_spec, g_spec],
        out_specs=(row_spec, row_spec),
        compiler_params=_params(("parallel",), 2 * tr * d * 14),
        name="resnorm",
    )(h, u, g_post.reshape(1, d), g_pre.reshape(1, d))


def _mm_kernel(*refs, nx):
    o_ref = refs[2 * nx]
    acc = _dot(refs[0][...], refs[nx][...])
    for t in range(1, nx):
        acc = acc + _dot(refs[t][...], refs[nx + t][...])
    o_ref[...] = acc.astype(o_ref.dtype)


def _mm_tiles(m, k, n, out_bytes):
    best = None
    col_step = MXU_WIDTH if n % MXU_WIDTH == 0 else LANE
    for bn in range(col_step, min(n, 4096) + 1, col_step):
        if n % bn:
            continue
        for bm in range(BF16_ROWS, min(m, 2048) + 1, BF16_ROWS):
            if m % bm:
                continue
            vm = 2 * bm * k * 2 + k * bn * 2 + 2 * bm * bn * out_bytes + bm * bn * 4
            if vm > MM_VMEM_BUDGET:
                continue
            score = bm * bn / (bm + bn)
            if best is None or score > best[0]:
                best = (score, bm, bn, vm)
    assert best is not None, (m, k, n)
    return best[1:]


def matmul(xs, w, layer, n_cols=None, out_dtype=F32):
    m, kp = xs[0].shape
    assert all(x.shape == (m, kp) for x in xs)
    n = w.shape[2] if n_cols is None else n_cols
    bm, bn, vm = _mm_tiles(m, kp * len(xs), n, jnp.dtype(out_dtype).itemsize)

    def w_spec(t):
        return pl.BlockSpec((None, kp, bn), lambda j, i: (layer, t, j), pipeline_mode=pl.Buffered(1))

    return pl.pallas_call(
        functools.partial(_mm_kernel, nx=len(xs)),
        out_shape=jax.ShapeDtypeStruct((m, n), out_dtype),
        grid=(n // bn, m // bm),
        in_specs=[pl.BlockSpec((bm, kp), lambda j, i: (i, 0)) for _ in xs] + [w_spec(t) for t in range(len(xs))],
        out_specs=pl.BlockSpec((bm, bn), lambda j, i: (i, j)),
        compiler_params=_params(("parallel", "parallel"), vm),
        name="matmul",
    )(*xs, *([w] * len(xs)))


def _ffn_up_kernel(x_ref, wg_ref, wv_ref, cw_ref, cb_ref, prev_ref, a_ref, st_ref, carry_ref,
                   *, nseq, rows, nblk):
    x = x_ref[...]
    tf = wg_ref.shape[1]
    if nblk > 1:
        first = (pl.program_id(1) % nblk) == 0

        @pl.when(first)
        def _():
            carry_ref[...] = jnp.zeros(carry_ref.shape, F32)

    t = lax.broadcasted_iota(jnp.int32, (1, rows, 1), 1)
    for c0 in range(0, tf, FF_SUB):
        cs = slice(c0, c0 + FF_SUB)
        g = _dot(x, wg_ref[:, cs])
        v = _dot(x, wv_ref[:, cs])
        if nblk == 1:
            prev = prev_ref[:, :, cs]
        else:
            prev = jnp.where(first, prev_ref[:, :, cs], carry_ref[SUBLANE - 2:SUBLANE, cs][None])
            carry_ref[:, cs] = g[rows - SUBLANE:rows, :]
        g3 = g.reshape(nseq, rows, FF_SUB)
        g1 = pltpu.roll(g, 1, axis=0).reshape(nseq, rows, FF_SUB)
        g2 = pltpu.roll(g, 2, axis=0).reshape(nseq, rows, FF_SUB)
        p0 = prev[:, 0:1, :]
        p1 = prev[:, 1:2, :]
        g1 = jnp.where(t == 0, p1, g1)
        g2 = jnp.where(t == 0, p0, jnp.where(t == 1, p1, g2))
        cw = cw_ref[:, cs]
        c = cw[0:1, :] * g2 + cw[1:2, :] * g1 + cw[2:3, :] * g3 + cb_ref[:, cs]
        a = jax.nn.gelu(c) * v.reshape(nseq, rows, FF_SUB)
        a_ref[:, cs] = a.reshape(nseq * rows, FF_SUB).astype(a_ref.dtype)
        st_ref[:, :, cs] = g3[:, rows - 2:rows, :]


def ffn_up(hn, w_up, conv_w, conv_b, layer, prev, seq_len):
    m, d = hn.shape
    f = w_up.shape[2] // 2
    nb = m // seq_len
    tf = FF_TILE
    nt = f // tf
    if seq_len >= 512:
        rows = _divisor_tile(seq_len, 1024, BF16_ROWS)
        nseq, nblk = 1, seq_len // rows
    else:
        rows, nblk = seq_len, 1
        nseq = _divisor_tile(nb, max(1, 1024 // seq_len), 1)
    bm = nseq * rows
    est = 2 * bm * d * 2 + 2 * d * tf * 2 + 2 * bm * tf * 2 + 10 * bm * FF_SUB * 4
    resident = dict(pipeline_mode=pl.Buffered(1))
    a, st = pl.pallas_call(
        functools.partial(_ffn_up_kernel, nseq=nseq, rows=rows, nblk=nblk),
        out_shape=(jax.ShapeDtypeStruct((m, f), BF16), jax.ShapeDtypeStruct((nb, F_CONV - 1, f), F32)),
        grid=(nt, m // bm),
        in_specs=[pl.BlockSpec((bm, d), lambda j, i: (i, 0)),
                  pl.BlockSpec((None, d, tf), lambda j, i: (layer, 0, j), **resident),
                  pl.BlockSpec((None, d, tf), lambda j, i: (layer, 0, nt + j), **resident),
                  pl.BlockSpec((None, F_CONV, tf), lambda j, i: (layer, 0, j)),
                  pl.BlockSpec((None, 1, tf), lambda j, i: (layer, 0, j)),
                  pl.BlockSpec((nseq, F_CONV - 1, tf), lambda j, i: (i // nblk, 0, j))],
        out_specs=(pl.BlockSpec((bm, tf), lambda j, i: (i, j)),
                   pl.BlockSpec((nseq, F_CONV - 1, tf), lambda j, i: (i // nblk, 0, j))),
        scratch_shapes=[pltpu.VMEM((SUBLANE, tf), F32)],
        compiler_params=_params(("parallel", "arbitrary"), est),
        name="ffn_up",
    )(hn, w_up, w_up, conv_w, conv_b, prev)
    return a, st


def _rope_tables(pos):
    half = ROT_DIM // 2
    inv = jnp.exp(-math.log(ROPE_THETA) * jnp.arange(half, dtype=F32) * 2.0 / ROT_DIM)
    ang = pos.astype(F32)[:, None] * inv[None, :]
    cos, sin = jnp.cos(ang), jnp.sin(ang)
    n = pos.shape[0]
    one = jnp.ones((n, A_HD - ROT_DIM), F32)
    zero = jnp.zeros((n, A_HD - ROT_DIM), F32)
    zh = jnp.zeros((n, half), F32)
    c = jnp.concatenate([cos, cos, one], axis=1)
    s_up = jnp.concatenate([-sin, zh, zero], axis=1)
    s_dn = jnp.concatenate([zh, sin, zero], axis=1)
    return tuple(jnp.concatenate([t, t], axis=1) for t in (c, s_up, s_dn))


def _rope(x, c, s_up, s_dn):
    half = ROT_DIM // 2
    return (x * c + pltpu.roll(x, 2 * A_HD - half, axis=1) * s_up + pltpu.roll(x, half, axis=1) * s_dn)


def _lambda(lq1, lk1, lq2, lk2, lambda_init):
    l1 = jnp.sum(lq1[...] * lk1[...], axis=-1, keepdims=True)
    l2 = jnp.sum(lq2[...] * lk2[...], axis=-1, keepdims=True)
    return jnp.exp(l1) - jnp.exp(l2) + lambda_init


def _two_softmax_pv(q, parts, lam):
    lane = lax.broadcasted_iota(jnp.int32, q.shape, 1)
    outs = []
    for sel in (lane < A_HD, lane >= A_HD):
        qs = jnp.where(sel, q, 0.0).astype(BF16)
        scores = []
        for kb, _, mask in parts:
            s = _dot_nt(qs, kb)
            if mask is not None:
                s = jnp.where(mask, s, -jnp.inf)
            scores.append(s)
        mx = scores[0].max(axis=-1, keepdims=True)
        for s in scores[1:]:
            mx = jnp.maximum(mx, s.max(axis=-1, keepdims=True))
        den = None
        num = None
        for s, (_, vb, _) in zip(scores, parts):
            p = jnp.exp(s - mx)
            d = jnp.sum(p, axis=-1, keepdims=True)
            o = _dot(p.astype(BF16), vb)
            den = d if den is None else den + d
            num = o if num is None else num + o
        outs.append(num / den)
    return outs[0] - lam * outs[1]


def _attn_prompt_kernel(q_ref, k_ref, v_ref, c_ref, su_ref, sd_ref, lq1, lk1, lq2, lk2, sub_ref,
                        o_ref, kr_ref, kb_ref, vb_ref, *, seq_len, pad_len, lambda_init):
    lam = _lambda(lq1, lk1, lq2, lk2, lambda_init)
    c, su, sd = c_ref[...], su_ref[...], sd_ref[...]
    k = _rope(k_ref[...], c, su, sd)
    kr_ref[...] = k
    kb_ref[0:seq_len, :] = k.astype(BF16)
    vb_ref[0:seq_len, :] = v_ref[...].astype(BF16)
    if pad_len > seq_len:
        kb_ref[seq_len:pad_len, :] = jnp.zeros((pad_len - seq_len, 2 * A_HD), BF16)
        vb_ref[seq_len:pad_len, :] = jnp.zeros((pad_len - seq_len, 2 * A_HD), BF16)
    qb = ATTN_QBLOCK
    shift = CHUNK - N_META
    for r0 in range(0, seq_len, qb):
        r1 = min(seq_len, r0 + qb)
        last_chunk = (r1 - 1 + shift) // CHUNK
        ke = min(pad_len, -(-(N_META + CHUNK * last_chunk) // LANE) * LANE)
        q = _rope(q_ref[r0:r1, :], c[r0:r1], su[r0:r1], sd[r0:r1]) * (A_HD ** -0.5)
        cq = (lax.broadcasted_iota(jnp.int32, (r1 - r0, 1), 0) + (r0 + shift)) // CHUNK
        ck = (lax.broadcasted_iota(jnp.int32, (1, ke), 1) + shift) // CHUNK
        o = _two_softmax_pv(q, [(kb_ref[0:ke, :], vb_ref[0:ke, :], cq >= ck)], lam)
        o_ref[r0:r1, :] = (_rms(o, sub_ref[...]) * (1.0 - lambda_init)).astype(o_ref.dtype)


def _attn_sample_kernel(q_ref, k_ref, v_ref, kp_ref, vp_ref, c_ref, su_ref, sd_ref, lq1, lk1, lq2, lk2,
                        sub_ref, o_ref, kr_ref, *, lambda_init):
    lam = _lambda(lq1, lk1, lq2, lk2, lambda_init)
    c, su, sd = c_ref[...], su_ref[...], sd_ref[...]
    k = _rope(k_ref[...], c, su, sd)
    kr_ref[...] = k
    q = _rope(q_ref[...], c, su, sd) * (A_HD ** -0.5)
    parts = [(kp_ref[0].astype(BF16), vp_ref[0].astype(BF16), None),
             (k.astype(BF16), v_ref[...].astype(BF16), None)]
    o = _two_softmax_pv(q, parts, lam)
    o_ref[...] = (_rms(o, sub_ref[...]) * (1.0 - lambda_init)).astype(o_ref.dtype)


def diff_attention(proj, pos, seq_len, n_heads, lams, subln, lambda_init, k_past=None, v_past=None):
    m = proj.shape[0]
    nb = m // seq_len
    hw = 2 * A_HD
    tabs = _rope_tables(pos)
    vec = lambda: pl.BlockSpec((1, A_HD), lambda b, h: (0, 0))
    tab = lambda: pl.BlockSpec((seq_len, hw), lambda b, h: (0, 0))
    col = lambda off: pl.BlockSpec((seq_len, hw), lambda b, h: (b, off + h))
    out_specs = (pl.BlockSpec((seq_len, hw), lambda b, h: (b, h)),
                 pl.BlockSpec((seq_len, hw), lambda b, h: (b, h)))
    out_shape = (jax.ShapeDtypeStruct((m, n_heads * hw), BF16), jax.ShapeDtypeStruct((m, n_heads * hw), F32))
    lam_args = [l.reshape(1, A_HD) for l in lams]
    sub = subln.reshape(1, hw)
    sub_spec = pl.BlockSpec((1, hw), lambda b, h: (0, 0))
    if k_past is None:
        pad_len = -(-seq_len // LANE) * LANE
        est = 2 * seq_len * hw * (3 * 4 + 3 * 4 + 2 + 4) + 2 * pad_len * hw * 2 + 8 * LANE * pad_len * 4
        return pl.pallas_call(
            functools.partial(_attn_prompt_kernel, seq_len=seq_len, pad_len=pad_len, lambda_init=lambda_init),
            out_shape=out_shape,
            grid=(nb, n_heads),
            in_specs=[col(0), col(n_heads), col(2 * n_heads), tab(), tab(), tab(),
                      vec(), vec(), vec(), vec(), sub_spec],
            out_specs=out_specs,
            scratch_shapes=[pltpu.VMEM((pad_len, hw), BF16), pltpu.VMEM((pad_len, hw), BF16)],
            compiler_params=_params(("parallel", "parallel"), est),
            name="attn_prompt",
        )(proj, proj, proj, *tabs, *lam_args, sub)
    past = k_past.shape[1]
    kp = k_past.reshape(nb, past, n_heads * hw)
    vp = v_past.reshape(nb, past, n_heads * hw)
    past_spec = lambda: pl.BlockSpec((1, past, hw), lambda b, h: (b, 0, h))
    est = 2 * (2 * past * hw * 4 + 8 * seq_len * hw * 4) + 8 * seq_len * past * 4
    return pl.pallas_call(
        functools.partial(_attn_sample_kernel, lambda_init=lambda_init),
        out_shape=out_shape,
        grid=(nb, n_heads),
        in_specs=[col(0), col(n_heads), col(2 * n_heads), past_spec(), past_spec(), tab(), tab(), tab(),
                  vec(), vec(), vec(), vec(), sub_spec],
        out_specs=out_specs,
        compiler_params=_params(("parallel", "parallel"), est),
        name="attn_sample",
    )(proj, proj, proj, kp, vp, *tabs, *lam_args, sub)


def _cumsum_rows(x):
    n = x.shape[0]
    row = lax.broadcasted_iota(jnp.int32, (n, 1), 0)
    sh = 1
    while sh < n:
        x = x + jnp.where(row >= sh, pltpu.roll(x, sh, axis=0), 0.0)
        sh *= 2
    return x


def _conv4_silu(ext, w, b, rows):
    acc = b + w[S_CONV - 1:S_CONV, :] * ext[SUBLANE:SUBLANE + rows, :]
    for j in range(S_CONV - 1):
        back = S_CONV - 1 - j
        acc = acc + w[j:j + 1, :] * pltpu.roll(ext, back, axis=0)[SUBLANE:SUBLANE + rows, :]
    return jax.nn.silu(acc)


def _ssd_kernel(xs_ref, bm_ref, cm_ref, z_ref, dt_ref, cwx_ref, cwb_ref, cwc_ref, cbx_ref, cbb_ref, cbc_ref,
                c0x_ref, c0b_ref, c0c_ref, dtb_ref, alog_ref, dsk_ref, nrm_ref, st0_ref,
                y_ref, stout_ref, s_ref, *, seq_len, chunk, has_state):
    hp = xs_ref.shape[1]
    nh = hp // S_HD
    if has_state:
        s_ref[...] = st0_ref[0].reshape(hp, S_N).T
    else:
        s_ref[...] = jnp.zeros((S_N, hp), F32)
    a_neg = -jnp.exp(alog_ref[...])
    dtb = dtb_ref[...]
    e_row = lax.broadcasted_iota(jnp.int32, (LANE, hp), 0)
    e_col = lax.broadcasted_iota(jnp.int32, (LANE, hp), 1) // S_HD
    expand = (e_row == e_col).astype(BF16)
    eye = (lax.broadcasted_iota(jnp.int32, (LANE, LANE), 0)
           == lax.broadcasted_iota(jnp.int32, (LANE, LANE), 1)).astype(BF16)
    lane_hp = lax.broadcasted_iota(jnp.int32, (1, 2 * S_HD), 1)

    def pad8(c0_ref):
        c0 = c0_ref[0]
        return jnp.concatenate([jnp.zeros((SUBLANE - (S_CONV - 1), c0.shape[1]), F32), c0], axis=0)

    def ext_of(ref, start, rows, c0_ref):
        if isinstance(start, int) and start == 0:
            return jnp.concatenate([pad8(c0_ref), ref[0:rows, :]], axis=0)
        return ref[pl.ds(start - SUBLANE, rows + SUBLANE), :]

    def to_heads(v):
        hi, mid, lo = _split3(v)
        return _dot(hi, expand) + _dot(mid, expand) + _dot(lo, expand)

    def step(start, rows):
        x = _conv4_silu(ext_of(xs_ref, start, rows, c0x_ref), cwx_ref[...], cbx_ref[...], rows)
        bmat = _conv4_silu(ext_of(bm_ref, start, rows, c0b_ref), cwb_ref[...], cbb_ref[...], rows)
        cmat = _conv4_silu(ext_of(cm_ref, start, rows, c0c_ref), cwc_ref[...], cbc_ref[...], rows)
        dt = jax.nn.softplus(dt_ref[pl.ds(start, rows), :] + dtb)
        cum = _cumsum_rows(dt * a_neg)
        hi, mid, lo = _split3(cum)
        cum_t = _dot_nt(eye, hi) + _dot_nt(eye, mid) + _dot_nt(eye, lo)
        cumx = to_heads(cum)
        xdt = x * to_heads(dt)
        xdt_b = xdt.astype(BF16)
        c_b = cmat.astype(BF16)
        b_b = bmat.astype(BF16)
        cb = _dot_nt(c_b, b_b)
        causal = (lax.broadcasted_iota(jnp.int32, (rows, rows), 0)
                  >= lax.broadcasted_iota(jnp.int32, (rows, rows), 1))
        cols = []
        for pair in range(nh // 2):
            xpair = xdt_b[:, pair * 2 * S_HD:(pair + 1) * 2 * S_HD]
            acc = None
            for sub in range(2):
                h = 2 * pair + sub
                seg = cum[:, h:h + 1] - cum_t[h:h + 1, :]
                wgt = (cb * jnp.exp(jnp.where(causal, seg, -jnp.inf))).astype(BF16)
                keep = (lane_hp < S_HD) if sub == 0 else (lane_hp >= S_HD)
                part = _dot(wgt, jnp.where(keep, xpair, jnp.zeros_like(xpair)))
                acc = part if acc is None else acc + part
            cols.append(acc)
        y = jnp.concatenate(cols, axis=1)
        s_old = s_ref[...]
        y = y + _dot(c_b, s_old.astype(BF16)) * jnp.exp(cumx)
        y = y + x * dsk_ref[...]
        last = cumx[rows - 1:rows, :]
        tail = jnp.exp(last - cumx)
        s_ref[...] = s_old * jnp.exp(last) + _dot_tn(b_b, (xdt * tail).astype(BF16))
        yz = y * jax.nn.silu(z_ref[pl.ds(start, rows), :])
        y_ref[pl.ds(start, rows), :] = _rms(yz, nrm_ref[...]).astype(y_ref.dtype)

    head = seq_len % chunk
    if head:
        step(0, head)
    n_full = seq_len // chunk
    if n_full == 1 and head == 0:
        step(0, chunk)
    elif n_full:
        def body(i, carry):
            step(pl.multiple_of(head + i * chunk, BF16_ROWS), chunk)
            return carry
        if head == 0:
            step(0, chunk)
            lax.fori_loop(1, n_full, body, 0)
        else:
            lax.fori_loop(0, n_full, body, 0)
    stout_ref[0] = s_ref[...].T.reshape(nh, S_HD, S_N)


def ssd_mixer(proj, dtp, x_off, z_off, seq_len, conv_w, conv_b, conv0, dt_bias, a_log, d_skip, s_norm, state0):
    m = proj.shape[0]
    nb = m // seq_len
    heads = a_log.shape[0]
    inner = heads * S_HD
    g = (conv_w.shape[1] - inner) // (2 * S_N)
    nh = heads // g
    hp = nh * S_HD
    chunk = min(SSD_CHUNK, seq_len)

    def per_head(v):
        return jnp.zeros((g, LANE), F32).at[:, :nh].set(v.reshape(g, nh)).reshape(1, g * LANE)

    xb, bb, cb = x_off // hp, (x_off + inner) // S_N, (x_off + inner + g * S_N) // S_N
    row = lambda width, blk: pl.BlockSpec((seq_len, width), lambda b, j: (b, blk + j))
    wrow = lambda rows, width, blk: pl.BlockSpec((rows, width), lambda b, j: (0, blk + j))
    c0 = lambda width, blk: pl.BlockSpec((1, S_CONV - 1, width), lambda b, j: (b, 0, blk + j))
    has_state = state0 is not None
    if not has_state:
        state0 = jnp.zeros((1, heads, S_HD, S_N), F32)
        st_spec = pl.BlockSpec((1, nh, S_HD, S_N), lambda b, j: (0, j, 0, 0))
    else:
        st_spec = pl.BlockSpec((1, nh, S_HD, S_N), lambda b, j: (b, j, 0, 0))
    cbias = conv_b.reshape(1, -1)
    ib, icb, icc = 0, inner // S_N, (inner + g * S_N) // S_N
    est = 2 * seq_len * (2 * hp * 4 + 3 * LANE * 4 + hp * 2) + 40 * chunk * hp * 4
    return pl.pallas_call(
        functools.partial(_ssd_kernel, seq_len=seq_len, chunk=chunk, has_state=has_state),
        out_shape=(jax.ShapeDtypeStruct((m, inner), BF16), jax.ShapeDtypeStruct((nb, heads, S_HD, S_N), F32)),
        grid=(nb, g),
        in_specs=[row(hp, xb), row(S_N, bb), row(S_N, cb), row(hp, z_off // hp),
                  pl.BlockSpec((seq_len, LANE), lambda b, j: (b, j)),
                  wrow(S_CONV, hp, ib), wrow(S_CONV, S_N, icb), wrow(S_CONV, S_N, icc),
                  wrow(1, hp, ib), wrow(1, S_N, icb), wrow(1, S_N, icc),
                  c0(hp, ib), c0(S_N, icb), c0(S_N, icc),
                  wrow(1, LANE, 0), wrow(1, LANE, 0), wrow(1, hp, 0), wrow(1, hp, 0), st_spec],
        out_specs=(pl.BlockSpec((seq_len, hp), lambda b, j: (b, j)),
                   pl.BlockSpec((1, nh, S_HD, S_N), lambda b, j: (b, j, 0, 0))),
        scratch_shapes=[pltpu.VMEM((S_N, hp), F32)],
        compiler_params=_params(("parallel", "parallel"), est),
        name="ssd",
    )(proj, proj, proj, proj, dtp, conv_w, conv_w, conv_w, cbias, cbias, cbias, conv0, conv0, conv0,
      per_head(dt_bias), per_head(a_log), jnp.repeat(d_skip, S_HD).reshape(1, inner), s_norm.reshape(1, inner),
      state0)


def _gla_kernel(q_ref, k_ref, v_ref, r_ref, gl_ref, w2_ref, bg_ref, gn_ref, s0_ref, o_ref, sout_ref,
                s_ref, cum_ref, qa_ref, ka_ref, qi_ref, kt_ref, of_ref, *, seq_len, chunk, has_state):
    if has_state:
        s_ref[...] = s0_ref[0, 0].T
    else:
        s_ref[...] = jnp.zeros(s_ref.shape, F32)
    scale = G_DK ** -0.5
    n_full = seq_len // chunk
    tail = seq_len - n_full * chunk
    assert tail & (tail - 1) == 0 and chunk & (chunk - 1) == 0

    def prep(start, rows, c):
        sl = pl.ds(start, rows)
        zg = _dot(gl_ref[sl, :].astype(BF16), w2_ref[...]) + bg_ref[...]
        cum = jax.nn.log_sigmoid(zg) / GATE_TAU
        pos = lax.broadcasted_iota(jnp.int32, (rows, 1), 0) & (c - 1)
        sh = 1
        while sh < c:
            cum = cum + jnp.where(pos >= sh, pltpu.roll(cum, sh, axis=0), 0.0)
            sh *= 2
        cum_ref[sl, :] = cum
        n = rows // c
        cum3 = cum.reshape(n, c, G_DK)
        anchor = cum3[:, c // 2 - 1:c // 2, :]
        last = cum3[:, c - 1:c, :]
        q3 = (q_ref[sl, :] * scale).reshape(n, c, G_DK)
        k3 = k_ref[sl, :].reshape(n, c, G_DK)
        flat = lambda a: a.reshape(rows, G_DK).astype(BF16)
        qa_ref[sl, :] = flat(q3 * jnp.exp(cum3 - anchor))
        ka_ref[sl, :] = flat(k3 * jnp.exp(anchor - cum3))
        qi_ref[sl, :] = flat(q3 * jnp.exp(cum3))
        kt_ref[sl, :] = flat(k3 * jnp.exp(last - cum3))

    def post(start, rows):
        sl = pl.ds(start, rows)
        o_ref[sl, :] = (_rms(of_ref[sl, :], gn_ref[...]) * jax.nn.silu(r_ref[sl, :])).astype(o_ref.dtype)

    def blocks(fn, *extra):
        main = n_full * chunk
        nblk = main // GLA_BLOCK
        if nblk:
            def body(i, carry):
                fn(pl.multiple_of(i * GLA_BLOCK, GLA_BLOCK), GLA_BLOCK, *extra)
                return carry
            lax.fori_loop(0, nblk, body, 0)
        if main > nblk * GLA_BLOCK:
            fn(nblk * GLA_BLOCK, main - nblk * GLA_BLOCK, *extra)

    blocks(prep, chunk)
    if tail:
        prep(n_full * chunk, tail, tail)

    def step(start, c):
        sl = pl.ds(start, c)
        causal = (lax.broadcasted_iota(jnp.int32, (c, c), 0) >= lax.broadcasted_iota(jnp.int32, (c, c), 1))
        att = jnp.where(causal, _dot_nt(qa_ref[sl, :], ka_ref[sl, :]), 0.0).astype(BF16)
        v_b = v_ref[sl, :].astype(BF16)
        s_old = s_ref[...]
        of_ref[sl, :] = _dot(att, v_b) + _dot_nt(qi_ref[sl, :], s_old.astype(BF16))
        last = cum_ref[pl.ds(start + c - SUBLANE, SUBLANE), :][SUBLANE - 1:SUBLANE, :]
        s_ref[...] = s_old * jnp.exp(last) + _dot_tn(v_b, kt_ref[sl, :])

    def body(i, carry):
        step(pl.multiple_of(i * chunk, chunk), chunk)
        return carry
    lax.fori_loop(0, n_full, body, 0, unroll=2 if n_full % 2 == 0 else 1)
    if tail:
        step(n_full * chunk, tail)

    blocks(post)
    if tail:
        post(n_full * chunk, tail)
    sout_ref[0, 0] = s_ref[...].T


def gla_mixer(proj, glow, seq_len, n_heads, w_gate2, b_gate, g_norm, state0):
    m = proj.shape[0]
    nb = m // seq_len
    kq, vq = n_heads * G_DK, n_heads * G_DV
    has_state = state0 is not None
    if not has_state:
        state0 = jnp.zeros((1, 1, G_DK, G_DV), F32)
        st_spec = pl.BlockSpec((1, 1, G_DK, G_DV), lambda b, h: (0, 0, 0, 0))
    else:
        st_spec = pl.BlockSpec((1, 1, G_DK, G_DV), lambda b, h: (b, h, 0, 0))
    w2 = jnp.zeros((LANE, kq), BF16).at[:G_RANK].set(w_gate2.astype(BF16))
    row = lambda width, blk: pl.BlockSpec((seq_len, width), lambda b, h: (b, blk + h))
    est = (2 * seq_len * (2 * G_DK * 4 + 2 * G_DV * 4 + LANE * 4 + G_DV * 2)
           + seq_len * (G_DK * 4 + 4 * G_DK * 2 + G_DV * 4) + 8 * G_DK * G_DV * 4)
    return pl.pallas_call(
        functools.partial(_gla_kernel, seq_len=seq_len, chunk=GLA_CHUNK, has_state=has_state),
        out_shape=(jax.ShapeDtypeStruct((m, vq), BF16), jax.ShapeDtypeStruct((nb, n_heads, G_DK, G_DV), F32)),
        grid=(nb, n_heads),
        in_specs=[row(G_DK, 0), row(G_DK, kq // G_DK), row(G_DV, 2 * kq // G_DV), row(G_DV, (2 * kq + vq) // G_DV),
                  pl.BlockSpec((seq_len, LANE), lambda b, h: (b, 0)),
                  pl.BlockSpec((LANE, G_DK), lambda b, h: (0, h)),
                  pl.BlockSpec((1, G_DK), lambda b, h: (0, h)),
                  pl.BlockSpec((1, G_DV), lambda b, h: (0, 0)),
                  st_spec],
        out_specs=(pl.BlockSpec((seq_len, G_DV), lambda b, h: (b, h)),
                   pl.BlockSpec((1, 1, G_DK, G_DV), lambda b, h: (b, h, 0, 0))),
        scratch_shapes=[pltpu.VMEM((G_DV, G_DK), F32), pltpu.VMEM((seq_len, G_DK), F32),
                        pltpu.VMEM((seq_len, G_DK), BF16), pltpu.VMEM((seq_len, G_DK), BF16),
                        pltpu.VMEM((seq_len, G_DK), BF16), pltpu.VMEM((seq_len, G_DK), BF16),
                        pltpu.VMEM((seq_len, G_DV), F32)],
        compiler_params=_params(("parallel", "parallel"), est),
        name="gla",
    )(proj, proj, proj, proj, glow, w2, b_gate.reshape(1, kq), g_norm.reshape(1, G_DV), state0)


def _prepare_weights(p, d_model):
    depth = p["w_up"].shape[0]
    n_even = p["w_in_even"].shape[0]
    heads_s = p["ssm_a_log"].shape[1]
    inner = heads_s * S_HD
    a_width = d_model // 2
    a_qk = a_width
    conv_dim = p["ssm_conv_w"].shape[2]
    groups = (conv_dim - inner) // (2 * S_N)
    main_e = 2 * a_qk + a_width + inner + conv_dim
    nh = heads_s // groups
    f = p["w_up"].shape[2] // 2
    fp = -(-f // FF_TILE) * FF_TILE
    kq = p["gla_w_gate2"].shape[2]
    main_o = 2 * kq + 2 * d_model
    w = {}
    w["in_even"] = p["w_in_even"].astype(BF16)
    dtw = p["w_in_even"][:, :, main_e:].reshape(n_even, d_model, groups, nh)
    w["dt"] = jnp.pad(dtw, ((0, 0), (0, 0), (0, 0), (0, LANE - nh))).reshape(n_even, d_model, groups * LANE).astype(BF16)
    w["out_even"] = p["w_out_even"].astype(BF16)
    w["in_odd"] = p["w_in_odd"].astype(BF16)
    w["glow"] = jnp.pad(p["w_in_odd"][:, :, main_o:], ((0, 0), (0, 0), (0, LANE - G_RANK))).astype(BF16)
    w["out_odd"] = p["w_out_odd"].astype(BF16)
    up = p["w_up"].astype(BF16).reshape(depth, d_model, 2, f)
    w["up"] = jnp.pad(up, ((0, 0), (0, 0), (0, 0), (0, fp - f))).reshape(depth, d_model, 2 * fp)
    w["down"] = jnp.pad(p["w_down"].astype(BF16), ((0, 0), (0, fp - f), (0, 0)))
    w["ffn_conv_w"] = jnp.pad(p["ffn_conv_w"], ((0, 0), (0, 0), (0, fp - f)))
    w["ffn_conv_b"] = jnp.pad(p["ffn_conv_b"], ((0, 0), (0, fp - f))).reshape(depth, 1, fp)
    w["f"], w["fp"], w["main_e"], w["main_o"] = f, fp, main_e, main_o
    return w


def _run_trunk(h, seq_len, pos, k_past, v_past, ssm0, sconv0, gla0, fconv0, p, w):
    rows, d = h.shape
    nb = rows // seq_len
    depth = p["w_up"].shape[0]
    a_heads = d // 2 // (2 * A_HD)
    a_w = a_heads * 2 * A_HD
    heads_s = p["ssm_a_log"].shape[1]
    inner = heads_s * S_HD
    conv_dim = p["ssm_conv_w"].shape[2]
    g_heads = d // G_DV
    f, fp = w["f"], w["fp"]
    ks, vs, ssms, sconvs, glas, fconvs = [], [], [], [], [], []
    hn = prenorm(h, p["norm_pre_mix"][0])
    for i in range(depth):
        if i % 2 == 0:
            e = i // 2
            lambda_init = 0.8 - 0.6 * math.exp(-0.3 * i)
            proj = matmul([hn], w["in_even"], e, n_cols=w["main_e"])
            dtp = matmul([hn], w["dt"], e)
            lams = (p["lambda_q1"][e], p["lambda_k1"][e], p["lambda_q2"][e], p["lambda_k2"][e])
            o_a, k_rot = diff_attention(proj, pos, seq_len, a_heads, lams, p["attn_subln"][e], lambda_init,
                                        None if k_past is None else k_past[e],
                                        None if v_past is None else v_past[e])
            conv0 = jnp.zeros((nb, S_CONV - 1, conv_dim), F32) if sconv0 is None else sconv0[e]
            y_s, ssm_new = ssd_mixer(proj, dtp, 3 * a_w + inner, 3 * a_w, seq_len, p["ssm_conv_w"][e],
                                     p["ssm_conv_b"][e], conv0, p["ssm_dt_bias"][e], p["ssm_a_log"][e],
                                     p["ssm_d"][e], p["ssm_norm"][e], None if ssm0 is None else ssm0[e])
            out = matmul([o_a, y_s], w["out_even"], e)
            proj3 = proj.reshape(nb, seq_len, -1)
            ks.append(k_rot)
            vs.append(proj[:, 2 * a_w:3 * a_w])
            ssms.append(ssm_new)
            sconvs.append(proj3[:, seq_len - (S_CONV - 1):, 3 * a_w + inner:3 * a_w + inner + conv_dim])
        else:
            o = i // 2
            proj = matmul([hn], w["in_odd"], o, n_cols=w["main_o"])
            glow = matmul([hn], w["glow"], o)
            y_g, gla_new = gla_mixer(proj, glow, seq_len, g_heads, p["gla_w_gate2"][o], p["gla_b_gate"][o],
                                     p["gla_norm"][o], None if gla0 is None else gla0[o])
            out = matmul([y_g], w["out_odd"], o)
            glas.append(gla_new)
        h, hn = residual_norm(h, out, p["norm_post_mix"][i], p["norm_pre_ffn"][i])
        prev = jnp.zeros((nb, F_CONV - 1, fp), F32) if fconv0 is None else jnp.pad(fconv0[i], ((0, 0), (0, 0), (0, fp - f)))
        a, fconv_new = ffn_up(hn, w["up"], w["ffn_conv_w"], w["ffn_conv_b"], i, prev, seq_len)
        fconvs.append(fconv_new[:, :, :f])
        ff = matmul([a], w["down"], i)
        h, hn = residual_norm(h, ff, p["norm_post_ffn"][i], p["norm_pre_mix"][i + 1] if i + 1 < depth else None)
    k_all = jnp.stack(ks).reshape(len(ks), nb, seq_len, 2 * a_heads, A_HD)
    v_all = jnp.stack(vs).reshape(len(vs), nb, seq_len, a_heads, 2 * A_HD)
    return (h, k_all, v_all, jnp.stack(ssms), jnp.stack(sconvs), jnp.stack(glas), jnp.stack(fconvs))


def kernel(x_prompt, x_sample, cache_k, cache_v, state_ssm, state_ssm_conv, state_gla, state_ffn_conv, meta_tokens, norm_pre_mix, norm_post_mix, norm_pre_ffn, norm_post_ffn, w_in_even, lambda_q1, lambda_k1, lambda_q2, lambda_k2, attn_subln, ssm_conv_w, ssm_conv_b, ssm_dt_bias, ssm_a_log, ssm_d, ssm_norm, w_out_even, w_in_odd, gla_w_gate2, gla_b_gate, gla_norm, w_out_odd, w_up, ffn_conv_w, ffn_conv_b, w_down):
    p = dict(norm_pre_mix=norm_pre_mix, norm_post_mix=norm_post_mix, norm_pre_ffn=norm_pre_ffn,
             norm_post_ffn=norm_post_ffn, w_in_even=w_in_even, lambda_q1=lambda_q1, lambda_k1=lambda_k1,
             lambda_q2=lambda_q2, lambda_k2=lambda_k2, attn_subln=attn_subln, ssm_conv_w=ssm_conv_w,
             ssm_conv_b=ssm_conv_b, ssm_dt_bias=ssm_dt_bias, ssm_a_log=ssm_a_log, ssm_d=ssm_d,
             ssm_norm=ssm_norm, w_out_even=w_out_even, w_in_odd=w_in_odd, gla_w_gate2=gla_w_gate2,
             gla_b_gate=gla_b_gate, gla_norm=gla_norm, w_out_odd=w_out_odd, w_up=w_up,
             ffn_conv_w=ffn_conv_w, ffn_conv_b=ffn_conv_b, w_down=w_down)
    b, seq, d = x_prompt.shape
    w = _prepare_weights(p, d)
    n_meta = meta_tokens.shape[0]
    lp = n_meta + seq
    h0 = jnp.concatenate([jnp.broadcast_to(meta_tokens[None], (b, n_meta, d)), x_prompt], axis=1).reshape(b * lp, d)
    hp, p_k, p_v, p_ssm, p_ssm_conv, p_gla, p_ffn_conv = _run_trunk(
        h0, lp, jnp.arange(lp), None, None, None, None, None, None, p, w)
    y_prompt = hp.reshape(b, lp, d)[:, n_meta:]
    bs, ls, _ = x_sample.shape
    pos_s = cache_k.shape[2] + jnp.arange(ls)
    ys, s_k, s_v, s_ssm, s_ssm_conv, s_gla, s_ffn_conv = _run_trunk(
        x_sample.reshape(bs * ls, d), ls, pos_s, cache_k, cache_v, state_ssm, state_ssm_conv, state_gla,
        state_ffn_conv, p, w)
    return (y_prompt, ys.reshape(bs, ls, d), p_k, p_v, p_ssm, p_ssm_conv, p_gla, p_ffn_conv,
            s_k, s_v, s_ssm, s_ssm_conv, s_gla, s_ffn_conv)
```

```python
import functools
import math

import jax
import jax.numpy as jnp
from jax import lax
from jax.experimental import pallas as pl
from jax.experimental.pallas import tpu as pltpu

F32 = jnp.float32
BF16 = jnp.bfloat16

CHUNK = 64
N_META = 16
EPS = 1e-6
A_HD = 64
ROT_DIM = A_HD // 4
ROPE_THETA = 500000.0
S_HD = 64
S_GROUPS = 4
S_N = 128
S_CONV = 4
G_DK = 256
G_DV = 512
G_RANK = 16
GATE_TAU = 16.0
F_CONV = 3

LANE = 128
SUBLANE = 8
BF16_ROWS = 16
MXU_WIDTH = 256
V7X_VMEM_BYTES = 64 * 1024 * 1024
VMEM_CAP = V7X_VMEM_BYTES - 8 * 1024 * 1024

MM_VMEM_BUDGET = 44 * 1024 * 1024

FF_TILE = 1024
FF_SUB = 256
ATTN_QBLOCK = 512
SSD_CHUNK = 128
GLA_CHUNK = 32
GLA_BLOCK = 256


def _params(semantics, vmem_estimate):
    limit = int(min(VMEM_CAP, max(32 * 1024 * 1024, vmem_estimate * 3 // 2)))
    return pltpu.CompilerParams(dimension_semantics=semantics, vmem_limit_bytes=limit)


def _divisor_tile(n, limit, mult):
    best = None
    for t in range(mult, min(n, limit) + 1, mult):
        if n % t == 0:
            best = t
    assert best is not None, (n, limit, mult)
    return best


def _rms(x, g):
    return x * lax.rsqrt(jnp.mean(x * x, axis=-1, keepdims=True) + EPS) * g


def _split3(x):
    hi = x.astype(BF16)
    r1 = x - hi.astype(F32)
    mid = r1.astype(BF16)
    lo = (r1 - mid.astype(F32)).astype(BF16)
    return hi, mid, lo


def _dot(a, b):
    return jnp.dot(a, b, preferred_element_type=F32)


def _dot_nt(a, b):
    return lax.dot_general(a, b, (((1,), (1,)), ((), ())), preferred_element_type=F32)


def _dot_tn(a, b):
    return lax.dot_general(a, b, (((0,), (0,)), ((), ())), preferred_element_type=F32)


def _prenorm_kernel(h_ref, g_ref, o_ref):
    o_ref[...] = _rms(h_ref[...], g_ref[...]).astype(o_ref.dtype)


def prenorm(h, g):
    rows, d = h.shape
    tr = _divisor_tile(rows, 256, BF16_ROWS)
    est = 2 * tr * d * (4 + 2)
    return pl.pallas_call(
        _prenorm_kernel,
        out_shape=jax.ShapeDtypeStruct((rows, d), BF16),
        grid=(rows // tr,),
        in_specs=[pl.BlockSpec((tr, d), lambda i: (i, 0)), pl.BlockSpec((1, d), lambda i: (0, 0))],
        out_specs=pl.BlockSpec((tr, d), lambda i: (i, 0)),
        compiler_params=_params(("parallel",), est),
        name="prenorm",
    )(h, g.reshape(1, d))


def _resnorm_kernel(h_ref, u_ref, gpost_ref, gpre_ref, hnew_ref, hn_ref):
    hnew = h_ref[...] + _rms(u_ref[...], gpost_ref[...])
    hnew_ref[...] = hnew
    hn_ref[...] = _rms(hnew, gpre_ref[...]).astype(hn_ref.dtype)


def _resnorm_last_kernel(h_ref, u_ref, gpost_ref, hnew_ref):
    hnew_ref[...] = h_ref[...] + _rms(u_ref[...], gpost_ref[...])


def residual_norm(h, u, g_post, g_pre):
    rows, d = h.shape
    tr = _divisor_tile(rows, 256, BF16_ROWS)
    row_spec = pl.BlockSpec((tr, d), lambda i: (i, 0))
    g_spec = pl.BlockSpec((1, d), lambda i: (0, 0))
    if g_pre is None:
        return pl.pallas_call(
            _resnorm_last_kernel,
            out_shape=jax.ShapeDtypeStruct((rows, d), F32),
            grid=(rows // tr,),
            in_specs=[row_spec, row_spec, g_spec],
            out_specs=row_spec,
            compiler_params=_params(("parallel",), 2 * tr * d * 12),
            name="resnorm_last",
        )(h, u, g_post.reshape(1, d)), None
    return pl.pallas_call(
        _resnorm_kernel,
        out_shape=(jax.ShapeDtypeStruct((rows, d), F32), jax.ShapeDtypeStruct((rows, d), BF16)),
        grid=(rows // tr,),
        in_specs=[row_spec, row_spec, g_spec, g_spec],
        out_specs=(row_spec, row_spec),
        compiler_params=_params(("parallel",), 2 * tr * d * 14),
        name="resnorm",
    )(h, u, g_post.reshape(1, d), g_pre.reshape(1, d))


def _mm_kernel(*refs, nx):
    o_ref = refs[2 * nx]
    acc = _dot(refs[0][...], refs[nx][...])
    for t in range(1, nx):
        acc = acc + _dot(refs[t][...], refs[nx + t][...])
    o_ref[...] = acc.astype(o_ref.dtype)


def _mm_tiles(m, k, n, out_bytes):
    def search(w_bufs):
        best = None
        col_step = MXU_WIDTH if n % MXU_WIDTH == 0 else LANE
        for bn in range(col_step, min(n, 4096) + 1, col_step):
            if n % bn:
                continue
            for bm in range(BF16_ROWS, min(m, 2048) + 1, BF16_ROWS):
                if m % bm:
                    continue
                vm = 2 * bm * k * 2 + w_bufs * k * bn * 2 + 2 * bm * bn * out_bytes + bm * bn * 4
                if vm > MM_VMEM_BUDGET:
                    continue
                score = bm * bn / (bm + bn)
                if best is None or score > best[0]:
                    best = (score, bm, bn, vm, w_bufs)
        return best

    single, double = search(1), search(2)
    assert single is not None, (m, k, n)
    best = double if double is not None and double[0] >= 0.9 * single[0] else single
    return best[1:]


def matmul(xs, w, layer, n_cols=None, out_dtype=F32):
    m, kp = xs[0].shape
    assert all(x.shape == (m, kp) for x in xs)
    n = w.shape[2] if n_cols is None else n_cols
    bm, bn, vm, w_bufs = _mm_tiles(m, kp * len(xs), n, jnp.dtype(out_dtype).itemsize)

    def w_spec(t):
        return pl.BlockSpec((None, kp, bn), lambda j, i: (layer, t, j), pipeline_mode=pl.Buffered(w_bufs))

    return pl.pallas_call(
        functools.partial(_mm_kernel, nx=len(xs)),
        out_shape=jax.ShapeDtypeStruct((m, n), out_dtype),
        grid=(n // bn, m // bm),
        in_specs=[pl.BlockSpec((bm, kp), lambda j, i: (i, 0)) for _ in xs] + [w_spec(t) for t in range(len(xs))],
        out_specs=pl.BlockSpec((bm, bn), lambda j, i: (i, j)),
        compiler_params=_params(("parallel", "parallel"), vm),
        name="matmul",
    )(*xs, *([w] * len(xs)))


def _cast_pad_cols_kernel(x_ref, o_ref):
    valid = x_ref.shape[1]
    o_ref[:, :valid] = x_ref[...].astype(o_ref.dtype)
    if o_ref.shape[1] > valid:
        o_ref[:, valid:] = jnp.zeros((o_ref.shape[0], o_ref.shape[1] - valid), o_ref.dtype)


def cast_pad_halves(w, fp):
    layers, d, f2 = w.shape
    f = f2 // 2
    assert f % LANE == 0 and fp % LANE == 0
    tr = _divisor_tile(d, 128, BF16_ROWS)
    return pl.pallas_call(
        _cast_pad_cols_kernel,
        out_shape=jax.ShapeDtypeStruct((layers, d, 2 * fp), BF16),
        grid=(layers, 2, d // tr),
        in_specs=[pl.BlockSpec((None, tr, f), lambda l, s, r: (l, r, s))],
        out_specs=pl.BlockSpec((None, tr, fp), lambda l, s, r: (l, r, s)),
        compiler_params=_params(("parallel", "parallel", "parallel"), 2 * tr * (f * 4 + fp * 2)),
        name="cast_pad_halves",
    )(w)


def _cast_pad_rows_kernel(x_ref, o_ref, *, n_valid):
    r = pl.program_id(1)

    @pl.when(r < n_valid)
    def _():
        o_ref[...] = x_ref[...].astype(o_ref.dtype)

    @pl.when(r >= n_valid)
    def _():
        o_ref[...] = jnp.zeros(o_ref.shape, o_ref.dtype)


def cast_pad_rows(w, fp):
    layers, f, d = w.shape
    tr = _divisor_tile(math.gcd(f, fp), 256, BF16_ROWS)
    n_valid = f // tr
    return pl.pallas_call(
        functools.partial(_cast_pad_rows_kernel, n_valid=n_valid),
        out_shape=jax.ShapeDtypeStruct((layers, fp, d), BF16),
        grid=(layers, fp // tr),
        in_specs=[pl.BlockSpec((None, tr, d), lambda l, r: (l, jnp.minimum(r, n_valid - 1), 0))],
        out_specs=pl.BlockSpec((None, tr, d), lambda l, r: (l, r, 0)),
        compiler_params=_params(("parallel", "parallel"), 2 * tr * d * 6),
        name="cast_pad_rows",
    )(w)


def _ffn_up_kernel(x_ref, wg_ref, wv_ref, cw_ref, cb_ref, prev_ref, a_ref, st_ref, carry_ref,
                   *, nseq, rows, nblk):
    x = x_ref[...]
    tf = wg_ref.shape[1]
    if nblk > 1:
        first = (pl.program_id(1) % nblk) == 0

        @pl.when(first)
        def _():
            carry_ref[...] = jnp.zeros(carry_ref.shape, F32)

    t = lax.broadcasted_iota(jnp.int32, (1, rows, 1), 1)
    for c0 in range(0, tf, FF_SUB):
        cs = slice(c0, c0 + FF_SUB)
        g = _dot(x, wg_ref[:, cs])
        v = _dot(x, wv_ref[:, cs])
        if nblk == 1:
            prev = prev_ref[:, :, cs]
        else:
            prev = jnp.where(first, prev_ref[:, :, cs], carry_ref[SUBLANE - 2:SUBLANE, cs][None])
            carry_ref[:, cs] = g[rows - SUBLANE:rows, :]
        g3 = g.reshape(nseq, rows, FF_SUB)
        g1 = pltpu.roll(g, 1, axis=0).reshape(nseq, rows, FF_SUB)
        g2 = pltpu.roll(g, 2, axis=0).reshape(nseq, rows, FF_SUB)
        p0 = prev[:, 0:1, :]
        p1 = prev[:, 1:2, :]
        g1 = jnp.where(t == 0, p1, g1)
        g2 = jnp.where(t == 0, p0, jnp.where(t == 1, p1, g2))
        cw = cw_ref[:, cs]
        c = cw[0:1, :] * g2 + cw[1:2, :] * g1 + cw[2:3, :] * g3 + cb_ref[:, cs]
        a = jax.nn.gelu(c) * v.reshape(nseq, rows, FF_SUB)
        a_ref[:, cs] = a.reshape(nseq * rows, FF_SUB).astype(a_ref.dtype)
        st_ref[:, :, cs] = g3[:, rows - 2:rows, :]


def ffn_up(hn, w_up, conv_w, conv_b, layer, prev, seq_len):
    m, d = hn.shape
    f = w_up.shape[2] // 2
    nb = m // seq_len
    tf = FF_TILE
    nt = f // tf
    if seq_len >= 512:
        rows = _divisor_tile(seq_len, 1024, BF16_ROWS)
        nseq, nblk = 1, seq_len // rows
    else:
        rows, nblk = seq_len, 1
        nseq = _divisor_tile(nb, max(1, 1024 // seq_len), 1)
    bm = nseq * rows
    est = 2 * bm * d * 2 + 2 * d * tf * 2 + 2 * bm * tf * 2 + 10 * bm * FF_SUB * 4
    resident = dict(pipeline_mode=pl.Buffered(1))
    a, st = pl.pallas_call(
        functools.partial(_ffn_up_kernel, nseq=nseq, rows=rows, nblk=nblk),
        out_shape=(jax.ShapeDtypeStruct((m, f), BF16), jax.ShapeDtypeStruct((nb, F_CONV - 1, f), F32)),
        grid=(nt, m // bm),
        in_specs=[pl.BlockSpec((bm, d), lambda j, i: (i, 0)),
                  pl.BlockSpec((None, d, tf), lambda j, i: (layer, 0, j), **resident),
                  pl.BlockSpec((None, d, tf), lambda j, i: (layer, 0, nt + j), **resident),
                  pl.BlockSpec((None, F_CONV, tf), lambda j, i: (layer, 0, j)),
                  pl.BlockSpec((None, 1, tf), lambda j, i: (layer, 0, j)),
                  pl.BlockSpec((nseq, F_CONV - 1, tf), lambda j, i: (i // nblk, 0, j))],
        out_specs=(pl.BlockSpec((bm, tf), lambda j, i: (i, j)),
                   pl.BlockSpec((nseq, F_CONV - 1, tf), lambda j, i: (i // nblk, 0, j))),
        scratch_shapes=[pltpu.VMEM((SUBLANE, tf), F32)],
        compiler_params=_params(("parallel", "arbitrary"), est),
        name="ffn_up",
    )(hn, w_up, w_up, conv_w, conv_b, prev)
    return a, st


def _rope_tables(pos):
    half = ROT_DIM // 2
    inv = jnp.exp(-math.log(ROPE_THETA) * jnp.arange(half, dtype=F32) * 2.0 / ROT_DIM)
    ang = pos.astype(F32)[:, None] * inv[None, :]
    cos, sin = jnp.cos(ang), jnp.sin(ang)
    n = pos.shape[0]
    one = jnp.ones((n, A_HD - ROT_DIM), F32)
    zero = jnp.zeros((n, A_HD - ROT_DIM), F32)
    zh = jnp.zeros((n, half), F32)
    c = jnp.concatenate([cos, cos, one], axis=1)
    s_up = jnp.concatenate([-sin, zh, zero], axis=1)
    s_dn = jnp.concatenate([zh, sin, zero], axis=1)
    return tuple(jnp.concatenate([t, t], axis=1) for t in (c, s_up, s_dn))


def _rope(x, c, s_up, s_dn):
    half = ROT_DIM // 2
    return (x * c + pltpu.roll(x, 2 * A_HD - half, axis=1) * s_up + pltpu.roll(x, half, axis=1) * s_dn)


def _lambda(lq1, lk1, lq2, lk2, lambda_init):
    l1 = jnp.sum(lq1[...] * lk1[...], axis=-1, keepdims=True)
    l2 = jnp.sum(lq2[...] * lk2[...], axis=-1, keepdims=True)
    return jnp.exp(l1) - jnp.exp(l2) + lambda_init


def _two_softmax_pv(q, parts, lam):
    lane = lax.broadcasted_iota(jnp.int32, q.shape, 1)
    outs = []
    for sel in (lane < A_HD, lane >= A_HD):
        qs = jnp.where(sel, q, 0.0).astype(BF16)
        scores = []
        for kb, _, mask in parts:
            s = _dot_nt(qs, kb)
            if mask is not None:
                s = jnp.where(mask, s, -jnp.inf)
            scores.append(s)
        mx = scores[0].max(axis=-1, keepdims=True)
        for s in scores[1:]:
            mx = jnp.maximum(mx, s.max(axis=-1, keepdims=True))
        den = None
        num = None
        for s, (_, vb, _) in zip(scores, parts):
            p = jnp.exp(s - mx)
            d = jnp.sum(p, axis=-1, keepdims=True)
            o = _dot(p.astype(BF16), vb)
            den = d if den is None else den + d
            num = o if num is None else num + o
        outs.append(num / den)
    return outs[0] - lam * outs[1]


def _attn_prompt_kernel(q_ref, k_ref, v_ref, c_ref, su_ref, sd_ref, lq1, lk1, lq2, lk2, sub_ref,
                        o_ref, kr_ref, kb_ref, vb_ref, *, seq_len, pad_len, lambda_init):
    lam = _lambda(lq1, lk1, lq2, lk2, lambda_init)
    c, su, sd = c_ref[...], su_ref[...], sd_ref[...]
    k = _rope(k_ref[...], c, su, sd)
    kr_ref[...] = k
    kb_ref[0:seq_len, :] = k.astype(BF16)
    vb_ref[0:seq_len, :] = v_ref[...].astype(BF16)
    if pad_len > seq_len:
        kb_ref[seq_len:pad_len, :] = jnp.zeros((pad_len - seq_len, 2 * A_HD), BF16)
        vb_ref[seq_len:pad_len, :] = jnp.zeros((pad_len - seq_len, 2 * A_HD), BF16)
    qb = ATTN_QBLOCK
    shift = CHUNK - N_META
    for r0 in range(0, seq_len, qb):
        r1 = min(seq_len, r0 + qb)
        last_chunk = (r1 - 1 + shift) // CHUNK
        ke = min(pad_len, -(-(N_META + CHUNK * last_chunk) // LANE) * LANE)
        q = _rope(q_ref[r0:r1, :], c[r0:r1], su[r0:r1], sd[r0:r1]) * (A_HD ** -0.5)
        cq = (lax.broadcasted_iota(jnp.int32, (r1 - r0, 1), 0) + (r0 + shift)) // CHUNK
        ck = (lax.broadcasted_iota(jnp.int32, (1, ke), 1) + shift) // CHUNK
        o = _two_softmax_pv(q, [(kb_ref[0:ke, :], vb_ref[0:ke, :], cq >= ck)], lam)
        o_ref[r0:r1, :] = (_rms(o, sub_ref[...]) * (1.0 - lambda_init)).astype(o_ref.dtype)


def _attn_sample_kernel(q_ref, k_ref, v_ref, kp_ref, vp_ref, c_ref, su_ref, sd_ref, lq1, lk1, lq2, lk2,
                        sub_ref, o_ref, kr_ref, *, lambda_init):
    lam = _lambda(lq1, lk1, lq2, lk2, lambda_init)
    c, su, sd = c_ref[...], su_ref[...], sd_ref[...]
    k = _rope(k_ref[...], c, su, sd)
    kr_ref[...] = k
    q = _rope(q_ref[...], c, su, sd) * (A_HD ** -0.5)
    parts = [(kp_ref[0].astype(BF16), vp_ref[0].astype(BF16), None),
             (k.astype(BF16), v_ref[...].astype(BF16), None)]
    o = _two_softmax_pv(q, parts, lam)
    o_ref[...] = (_rms(o, sub_ref[...]) * (1.0 - lambda_init)).astype(o_ref.dtype)


def diff_attention(proj, pos, seq_len, n_heads, lams, subln, lambda_init, k_past=None, v_past=None, layer=0):
    m = proj.shape[0]
    nb = m // seq_len
    hw = 2 * A_HD
    tabs = _rope_tables(pos)
    vec = lambda: pl.BlockSpec((1, A_HD), lambda b, h: (0, 0))
    tab = lambda: pl.BlockSpec((seq_len, hw), lambda b, h: (0, 0))
    col = lambda off: pl.BlockSpec((seq_len, hw), lambda b, h: (b, off + h))
    out_specs = (pl.BlockSpec((seq_len, hw), lambda b, h: (b, h)),
                 pl.BlockSpec((seq_len, hw), lambda b, h: (b, h)))
    out_shape = (jax.ShapeDtypeStruct((m, n_heads * hw), BF16), jax.ShapeDtypeStruct((m, n_heads * hw), F32))
    lam_args = [l.reshape(1, A_HD) for l in lams]
    sub = subln.reshape(1, hw)
    sub_spec = pl.BlockSpec((1, hw), lambda b, h: (0, 0))
    if k_past is None:
        pad_len = -(-seq_len // LANE) * LANE
        est = 2 * seq_len * hw * (3 * 4 + 3 * 4 + 2 + 4) + 2 * pad_len * hw * 2 + 8 * LANE * pad_len * 4
        return pl.pallas_call(
            functools.partial(_attn_prompt_kernel, seq_len=seq_len, pad_len=pad_len, lambda_init=lambda_init),
            out_shape=out_shape,
            grid=(nb, n_heads),
            in_specs=[col(0), col(n_heads), col(2 * n_heads), tab(), tab(), tab(),
                      vec(), vec(), vec(), vec(), sub_spec],
            out_specs=out_specs,
            scratch_shapes=[pltpu.VMEM((pad_len, hw), BF16), pltpu.VMEM((pad_len, hw), BF16)],
            compiler_params=_params(("parallel", "parallel"), est),
            name="attn_prompt",
        )(proj, proj, proj, *tabs, *lam_args, sub)
    past = k_past.shape[2]
    kp, vp = k_past, v_past
    past_spec = lambda: pl.BlockSpec((None, 1, past, hw), lambda b, h: (layer, b, 0, h))
    est = 2 * (2 * past * hw * 4 + 8 * seq_len * hw * 4) + 8 * seq_len * past * 4
    return pl.pallas_call(
        functools.partial(_attn_sample_kernel, lambda_init=lambda_init),
        out_shape=out_shape,
        grid=(nb, n_heads),
        in_specs=[col(0), col(n_heads), col(2 * n_heads), past_spec(), past_spec(), tab(), tab(), tab(),
                  vec(), vec(), vec(), vec(), sub_spec],
        out_specs=out_specs,
        compiler_params=_params(("parallel", "parallel"), est),
        name="attn_sample",
    )(proj, proj, proj, kp, vp, *tabs, *lam_args, sub)


def _cumsum_rows(x):
    n = x.shape[0]
    row = lax.broadcasted_iota(jnp.int32, (n, 1), 0)
    sh = 1
    while sh < n:
        x = x + jnp.where(row >= sh, pltpu.roll(x, sh, axis=0), 0.0)
        sh *= 2
    return x


def _conv4_silu(ext, w, b, rows):
    acc = b + w[S_CONV - 1:S_CONV, :] * ext[SUBLANE:SUBLANE + rows, :]
    for j in range(S_CONV - 1):
        back = S_CONV - 1 - j
        acc = acc + w[j:j + 1, :] * pltpu.roll(ext, back, axis=0)[SUBLANE:SUBLANE + rows, :]
    return jax.nn.silu(acc)


def _ssd_kernel(xs_ref, bm_ref, cm_ref, z_ref, dt_ref, cwx_ref, cwb_ref, cwc_ref, cbx_ref, cbb_ref, cbc_ref,
                c0x_ref, c0b_ref, c0c_ref, dtb_ref, alog_ref, dsk_ref, nrm_ref, st0_ref,
                y_ref, stout_ref, s_ref, *, seq_len, chunk, has_state):
    hp = xs_ref.shape[1]
    nh = hp // S_HD
    if has_state:
        s_ref[...] = st0_ref[0].reshape(hp, S_N).T
    else:
        s_ref[...] = jnp.zeros((S_N, hp), F32)
    a_neg = -jnp.exp(alog_ref[...])
    dtb = dtb_ref[...]
    e_row = lax.broadcasted_iota(jnp.int32, (LANE, hp), 0)
    e_col = lax.broadcasted_iota(jnp.int32, (LANE, hp), 1) // S_HD
    expand = (e_row == e_col).astype(BF16)
    eye = (lax.broadcasted_iota(jnp.int32, (LANE, LANE), 0)
           == lax.broadcasted_iota(jnp.int32, (LANE, LANE), 1)).astype(BF16)
    lane_hp = lax.broadcasted_iota(jnp.int32, (1, 2 * S_HD), 1)

    def pad8(c0_ref):
        c0 = c0_ref[0]
        return jnp.concatenate([jnp.zeros((SUBLANE - (S_CONV - 1), c0.shape[1]), F32), c0], axis=0)

    def ext_of(ref, start, rows, c0_ref):
        if isinstance(start, int) and start == 0:
            return jnp.concatenate([pad8(c0_ref), ref[0:rows, :]], axis=0)
        return ref[pl.ds(start - SUBLANE, rows + SUBLANE), :]

    def to_heads(v):
        hi, mid, lo = _split3(v)
        return _dot(hi, expand) + _dot(mid, expand) + _dot(lo, expand)

    def step(start, rows):
        x = _conv4_silu(ext_of(xs_ref, start, rows, c0x_ref), cwx_ref[...], cbx_ref[...], rows)
        bmat = _conv4_silu(ext_of(bm_ref, start, rows, c0b_ref), cwb_ref[...], cbb_ref[...], rows)
        cmat = _conv4_silu(ext_of(cm_ref, start, rows, c0c_ref), cwc_ref[...], cbc_ref[...], rows)
        dt = jax.nn.softplus(dt_ref[pl.ds(start, rows), :] + dtb)
        cum = _cumsum_rows(dt * a_neg)
        hi, mid, lo = _split3(cum)
        cum_t = _dot_nt(eye, hi) + _dot_nt(eye, mid) + _dot_nt(eye, lo)
        cumx = to_heads(cum)
        xdt = x * to_heads(dt)
        xdt_b = xdt.astype(BF16)
        c_b = cmat.astype(BF16)
        b_b = bmat.astype(BF16)
        cb = _dot_nt(c_b, b_b)
        causal = (lax.broadcasted_iota(jnp.int32, (rows, rows), 0)
                  >= lax.broadcasted_iota(jnp.int32, (rows, rows), 1))
        cols = []
        for pair in range(nh // 2):
            xpair = xdt_b[:, pair * 2 * S_HD:(pair + 1) * 2 * S_HD]
            acc = None
            for sub in range(2):
                h = 2 * pair + sub
                seg = cum[:, h:h + 1] - cum_t[h:h + 1, :]
                wgt = (cb * jnp.exp(jnp.where(causal, seg, -jnp.inf))).astype(BF16)
                keep = (lane_hp < S_HD) if sub == 0 else (lane_hp >= S_HD)
                part = _dot(wgt, jnp.where(keep, xpair, jnp.zeros_like(xpair)))
                acc = part if acc is None else acc + part
            cols.append(acc)
        y = jnp.concatenate(cols, axis=1)
        s_old = s_ref[...]
        y = y + _dot(c_b, s_old.astype(BF16)) * jnp.exp(cumx)
        y = y + x * dsk_ref[...]
        last = cumx[rows - 1:rows, :]
        tail = jnp.exp(last - cumx)
        s_ref[...] = s_old * jnp.exp(last) + _dot_tn(b_b, (xdt * tail).astype(BF16))
        yz = y * jax.nn.silu(z_ref[pl.ds(start, rows), :])
        y_ref[pl.ds(start, rows), :] = _rms(yz, nrm_ref[...]).astype(y_ref.dtype)

    head = seq_len % chunk
    if head:
        step(0, head)
    n_full = seq_len // chunk
    if n_full == 1 and head == 0:
        step(0, chunk)
    elif n_full:
        def body(i, carry):
            step(pl.multiple_of(head + i * chunk, BF16_ROWS), chunk)
            return carry
        if head == 0:
            step(0, chunk)
            lax.fori_loop(1, n_full, body, 0)
        else:
            lax.fori_loop(0, n_full, body, 0)
    stout_ref[0] = s_ref[...].T.reshape(nh, S_HD, S_N)


def ssd_mixer(proj, dtp, x_off, z_off, seq_len, conv_w, conv_b, conv0, dt_bias, a_log, d_skip, s_norm, state0):
    m = proj.shape[0]
    nb = m // seq_len
    heads = a_log.shape[0]
    inner = heads * S_HD
    g = (conv_w.shape[1] - inner) // (2 * S_N)
    nh = heads // g
    hp = nh * S_HD
    chunk = min(SSD_CHUNK, seq_len)

    def per_head(v):
        return jnp.zeros((g, LANE), F32).at[:, :nh].set(v.reshape(g, nh)).reshape(1, g * LANE)

    xb, bb, cb = x_off // hp, (x_off + inner) // S_N, (x_off + inner + g * S_N) // S_N
    row = lambda width, blk: pl.BlockSpec((seq_len, width), lambda b, j: (b, blk + j))
    wrow = lambda rows, width, blk: pl.BlockSpec((rows, width), lambda b, j: (0, blk + j))
    c0 = lambda width, blk: pl.BlockSpec((1, S_CONV - 1, width), lambda b, j: (b, 0, blk + j))
    has_state = state0 is not None
    if not has_state:
        state0 = jnp.zeros((1, heads, S_HD, S_N), F32)
        st_spec = pl.BlockSpec((1, nh, S_HD, S_N), lambda b, j: (0, j, 0, 0))
    else:
        st_spec = pl.BlockSpec((1, nh, S_HD, S_N), lambda b, j: (b, j, 0, 0))
    cbias = conv_b.reshape(1, -1)
    ib, icb, icc = 0, inner // S_N, (inner + g * S_N) // S_N
    est = 2 * seq_len * (2 * hp * 4 + 3 * LANE * 4 + hp * 2) + 40 * chunk * hp * 4
    return pl.pallas_call(
        functools.partial(_ssd_kernel, seq_len=seq_len, chunk=chunk, has_state=has_state),
        out_shape=(jax.ShapeDtypeStruct((m, inner), BF16), jax.ShapeDtypeStruct((nb, heads, S_HD, S_N), F32)),
        grid=(nb, g),
        in_specs=[row(hp, xb), row(S_N, bb), row(S_N, cb), row(hp, z_off // hp),
                  pl.BlockSpec((seq_len, LANE), lambda b, j: (b, j)),
                  wrow(S_CONV, hp, ib), wrow(S_CONV, S_N, icb), wrow(S_CONV, S_N, icc),
                  wrow(1, hp, ib), wrow(1, S_N, icb), wrow(1, S_N, icc),
                  c0(hp, ib), c0(S_N, icb), c0(S_N, icc),
                  wrow(1, LANE, 0), wrow(1, LANE, 0), wrow(1, hp, 0), wrow(1, hp, 0), st_spec],
        out_specs=(pl.BlockSpec((seq_len, hp), lambda b, j: (b, j)),
                   pl.BlockSpec((1, nh, S_HD, S_N), lambda b, j: (b, j, 0, 0))),
        scratch_shapes=[pltpu.VMEM((S_N, hp), F32)],
        compiler_params=_params(("parallel", "parallel"), est),
        name="ssd",
    )(proj, proj, proj, proj, dtp, conv_w, conv_w, conv_w, cbias, cbias, cbias, conv0, conv0, conv0,
      per_head(dt_bias), per_head(a_log), jnp.repeat(d_skip, S_HD).reshape(1, inner), s_norm.reshape(1, inner),
      state0)


def _gla_kernel(q_ref, k_ref, v_ref, r_ref, gl_ref, w2_ref, bg_ref, gn_ref, s0_ref, o_ref, sout_ref,
                s_ref, cum_ref, qa_ref, ka_ref, qi_ref, kt_ref, of_ref, *, seq_len, chunk, has_state):
    if has_state:
        s_ref[...] = s0_ref[0, 0].T
    else:
        s_ref[...] = jnp.zeros(s_ref.shape, F32)
    scale = G_DK ** -0.5
    n_full = seq_len // chunk
    tail = seq_len - n_full * chunk
    assert tail & (tail - 1) == 0 and chunk & (chunk - 1) == 0

    def prep(start, rows, c):
        sl = pl.ds(start, rows)
        zg = _dot(gl_ref[sl, :].astype(BF16), w2_ref[...]) + bg_ref[...]
        cum = jax.nn.log_sigmoid(zg) / GATE_TAU
        pos = lax.broadcasted_iota(jnp.int32, (rows, 1), 0) & (c - 1)
        sh = 1
        while sh < c:
            cum = cum + jnp.where(pos >= sh, pltpu.roll(cum, sh, axis=0), 0.0)
            sh *= 2
        cum_ref[sl, :] = cum
        n = rows // c
        cum3 = cum.reshape(n, c, G_DK)
        anchor = cum3[:, c // 2 - 1:c // 2, :]
        last = cum3[:, c - 1:c, :]
        q3 = (q_ref[sl, :] * scale).reshape(n, c, G_DK)
        k3 = k_ref[sl, :].reshape(n, c, G_DK)
        flat = lambda a: a.reshape(rows, G_DK).astype(BF16)
        qa_ref[sl, :] = flat(q3 * jnp.exp(cum3 - anchor))
        ka_ref[sl, :] = flat(k3 * jnp.exp(anchor - cum3))
        qi_ref[sl, :] = flat(q3 * jnp.exp(cum3))
        kt_ref[sl, :] = flat(k3 * jnp.exp(last - cum3))

    def post(start, rows):
        sl = pl.ds(start, rows)
        o_ref[sl, :] = (_rms(of_ref[sl, :], gn_ref[...]) * jax.nn.silu(r_ref[sl, :])).astype(o_ref.dtype)

    def blocks(fn, *extra):
        main = n_full * chunk
        nblk = main // GLA_BLOCK
        if nblk:
            def body(i, carry):
                fn(pl.multiple_of(i * GLA_BLOCK, GLA_BLOCK), GLA_BLOCK, *extra)
                return carry
            lax.fori_loop(0, nblk, body, 0)
        if main > nblk * GLA_BLOCK:
            fn(nblk * GLA_BLOCK, main - nblk * GLA_BLOCK, *extra)

    blocks(prep, chunk)
    if tail:
        prep(n_full * chunk, tail, tail)

    def step(start, c):
        sl = pl.ds(start, c)
        causal = (lax.broadcasted_iota(jnp.int32, (c, c), 0) >= lax.broadcasted_iota(jnp.int32, (c, c), 1))
        att = jnp.where(causal, _dot_nt(qa_ref[sl, :], ka_ref[sl, :]), 0.0).astype(BF16)
        v_b = v_ref[sl, :].astype(BF16)
        s_old = s_ref[...]
        of_ref[sl, :] = _dot(att, v_b) + _dot_nt(qi_ref[sl, :], s_old.astype(BF16))
        last = cum_ref[pl.ds(start + c - SUBLANE, SUBLANE), :][SUBLANE - 1:SUBLANE, :]
        s_ref[...] = s_old * jnp.exp(last) + _dot_tn(v_b, kt_ref[sl, :])

    def body(i, carry):
        step(pl.multiple_of(i * chunk, chunk), chunk)
        return carry
    lax.fori_loop(0, n_full, body, 0, unroll=4 if n_full % 4 == 0 else 1)
    if tail:
        step(n_full * chunk, tail)

    blocks(post)
    if tail:
        post(n_full * chunk, tail)
    sout_ref[0, 0] = s_ref[...].T


def gla_mixer(proj, glow, seq_len, n_heads, w_gate2, b_gate, g_norm, state0):
    m = proj.shape[0]
    nb = m // seq_len
    kq, vq = n_heads * G_DK, n_heads * G_DV
    has_state = state0 is not None
    if not has_state:
        state0 = jnp.zeros((1, 1, G_DK, G_DV), F32)
        st_spec = pl.BlockSpec((1, 1, G_DK, G_DV), lambda b, h: (0, 0, 0, 0))
    else:
        st_spec = pl.BlockSpec((1, 1, G_DK, G_DV), lambda b, h: (b, h, 0, 0))
    w2 = jnp.zeros((LANE, kq), BF16).at[:G_RANK].set(w_gate2.astype(BF16))
    row = lambda width, blk: pl.BlockSpec((seq_len, width), lambda b, h: (b, blk + h))
    est = (2 * seq_len * (2 * G_DK * 4 + 2 * G_DV * 4 + LANE * 4 + G_DV * 2)
           + seq_len * (G_DK * 4 + 4 * G_DK * 2 + G_DV * 4) + 8 * G_DK * G_DV * 4)
    return pl.pallas_call(
        functools.partial(_gla_kernel, seq_len=seq_len, chunk=GLA_CHUNK, has_state=has_state),
        out_shape=(jax.ShapeDtypeStruct((m, vq), BF16), jax.ShapeDtypeStruct((nb, n_heads, G_DK, G_DV), F32)),
        grid=(nb, n_heads),
        in_specs=[row(G_DK, 0), row(G_DK, kq // G_DK), row(G_DV, 2 * kq // G_DV), row(G_DV, (2 * kq + vq) // G_DV),
                  pl.BlockSpec((seq_len, LANE), lambda b, h: (b, 0)),
                  pl.BlockSpec((LANE, G_DK), lambda b, h: (0, h)),
                  pl.BlockSpec((1, G_DK), lambda b, h: (0, h)),
                  pl.BlockSpec((1, G_DV), lambda b, h: (0, 0)),
                  st_spec],
        out_specs=(pl.BlockSpec((seq_len, G_DV), lambda b, h: (b, h)),
                   pl.BlockSpec((1, 1, G_DK, G_DV), lambda b, h: (b, h, 0, 0))),
        scratch_shapes=[pltpu.VMEM((G_DV, G_DK), F32), pltpu.VMEM((seq_len, G_DK), F32),
                        pltpu.VMEM((seq_len, G_DK), BF16), pltpu.VMEM((seq_len, G_DK), BF16),
                        pltpu.VMEM((seq_len, G_DK), BF16), pltpu.VMEM((seq_len, G_DK), BF16),
                        pltpu.VMEM((seq_len, G_DV), F32)],
        compiler_params=_params(("parallel", "parallel"), est),
        name="gla",
    )(proj, proj, proj, proj, glow, w2, b_gate.reshape(1, kq), g_norm.reshape(1, G_DV), state0)


def _prepare_weights(p, d_model):
    depth = p["w_up"].shape[0]
    n_even = p["w_in_even"].shape[0]
    heads_s = p["ssm_a_log"].shape[1]
    inner = heads_s * S_HD
    a_width = d_model // 2
    a_qk = a_width
    conv_dim = p["ssm_conv_w"].shape[2]
    groups = (conv_dim - inner) // (2 * S_N)
    main_e = 2 * a_qk + a_width + inner + conv_dim
    nh = heads_s // groups
    f = p["w_up"].shape[2] // 2
    fp = -(-f // FF_TILE) * FF_TILE
    kq = p["gla_w_gate2"].shape[2]
    main_o = 2 * kq + 2 * d_model
    w = {}
    w["in_even"] = p["w_in_even"].astype(BF16)
    dtw = p["w_in_even"][:, :, main_e:].reshape(n_even, d_model, groups, nh)
    w["dt"] = jnp.pad(dtw, ((0, 0), (0, 0), (0, 0), (0, LANE - nh))).reshape(n_even, d_model, groups * LANE).astype(BF16)
    w["out_even"] = p["w_out_even"].astype(BF16)
    w["in_odd"] = p["w_in_odd"].astype(BF16)
    w["glow"] = jnp.pad(p["w_in_odd"][:, :, main_o:], ((0, 0), (0, 0), (0, LANE - G_RANK))).astype(BF16)
    w["out_odd"] = p["w_out_odd"].astype(BF16)
    w["up"] = cast_pad_halves(p["w_up"], fp)
    w["down"] = cast_pad_rows(p["w_down"], fp)
    w["ffn_conv_w"] = jnp.pad(p["ffn_conv_w"], ((0, 0), (0, 0), (0, fp - f)))
    w["ffn_conv_b"] = jnp.pad(p["ffn_conv_b"], ((0, 0), (0, fp - f))).reshape(depth, 1, fp)
    w["f"], w["fp"], w["main_e"], w["main_o"] = f, fp, main_e, main_o
    return w


def _run_trunk(h, seq_len, pos, k_past, v_past, ssm0, sconv0, gla0, fconv0, p, w):
    rows, d = h.shape
    nb = rows // seq_len
    depth = p["w_up"].shape[0]
    a_heads = d // 2 // (2 * A_HD)
    a_w = a_heads * 2 * A_HD
    heads_s = p["ssm_a_log"].shape[1]
    inner = heads_s * S_HD
    conv_dim = p["ssm_conv_w"].shape[2]
    g_heads = d // G_DV
    f, fp = w["f"], w["fp"]
    ks, vs, ssms, sconvs, glas, fconvs = [], [], [], [], [], []
    hn = prenorm(h, p["norm_pre_mix"][0])
    for i in range(depth):
        if i % 2 == 0:
            e = i // 2
            lambda_init = 0.8 - 0.6 * math.exp(-0.3 * i)
            proj = matmul([hn], w["in_even"], e, n_cols=w["main_e"])
            dtp = matmul([hn], w["dt"], e)
            lams = (p["lambda_q1"][e], p["lambda_k1"][e], p["lambda_q2"][e], p["lambda_k2"][e])
            o_a, k_rot = diff_attention(proj, pos, seq_len, a_heads, lams, p["attn_subln"][e], lambda_init,
                                        k_past, v_past, e)
            conv0 = jnp.zeros((nb, S_CONV - 1, conv_dim), F32) if sconv0 is None else sconv0[e]
            y_s, ssm_new = ssd_mixer(proj, dtp, 3 * a_w + inner, 3 * a_w, seq_len, p["ssm_conv_w"][e],
                                     p["ssm_conv_b"][e], conv0, p["ssm_dt_bias"][e], p["ssm_a_log"][e],
                                     p["ssm_d"][e], p["ssm_norm"][e], None if ssm0 is None else ssm0[e])
            out = matmul([o_a, y_s], w["out_even"], e)
            proj3 = proj.reshape(nb, seq_len, -1)
            ks.append(k_rot.reshape(nb, seq_len, 2 * a_heads, A_HD))
            vs.append(proj3[:, :, 2 * a_w:3 * a_w].reshape(nb, seq_len, a_heads, 2 * A_HD))
            ssms.append(ssm_new)
            sconvs.append(proj3[:, seq_len - (S_CONV - 1):, 3 * a_w + inner:3 * a_w + inner + conv_dim])
        else:
            o = i // 2
            proj = matmul([hn], w["in_odd"], o, n_cols=w["main_o"])
            glow = matmul([hn], w["glow"], o)
            y_g, gla_new = gla_mixer(proj, glow, seq_len, g_heads, p["gla_w_gate2"][o], p["gla_b_gate"][o],
                                     p["gla_norm"][o], None if gla0 is None else gla0[o])
            out = matmul([y_g], w["out_odd"], o)
            glas.append(gla_new)
        h, hn = residual_norm(h, out, p["norm_post_mix"][i], p["norm_pre_ffn"][i])
        prev = jnp.zeros((nb, F_CONV - 1, fp), F32) if fconv0 is None else jnp.pad(fconv0[i], ((0, 0), (0, 0), (0, fp - f)))
        a, fconv_new = ffn_up(hn, w["up"], w["ffn_conv_w"], w["ffn_conv_b"], i, prev, seq_len)
        fconvs.append(fconv_new[:, :, :f])
        ff = matmul([a], w["down"], i)
        h, hn = residual_norm(h, ff, p["norm_post_ffn"][i], p["norm_pre_mix"][i + 1] if i + 1 < depth else None)
    return (h, jnp.stack(ks), jnp.stack(vs), jnp.stack(ssms), jnp.stack(sconvs), jnp.stack(glas), jnp.stack(fconvs))


def kernel(x_prompt, x_sample, cache_k, cache_v, state_ssm, state_ssm_conv, state_gla, state_ffn_conv, meta_tokens, norm_pre_mix, norm_post_mix, norm_pre_ffn, norm_post_ffn, w_in_even, lambda_q1, lambda_k1, lambda_q2, lambda_k2, attn_subln, ssm_conv_w, ssm_conv_b, ssm_dt_bias, ssm_a_log, ssm_d, ssm_norm, w_out_even, w_in_odd, gla_w_gate2, gla_b_gate, gla_norm, w_out_odd, w_up, ffn_conv_w, ffn_conv_b, w_down):
    p = dict(norm_pre_mix=norm_pre_mix, norm_post_mix=norm_post_mix, norm_pre_ffn=norm_pre_ffn,
             norm_post_ffn=norm_post_ffn, w_in_even=w_in_even, lambda_q1=lambda_q1, lambda_k1=lambda_k1,
             lambda_q2=lambda_q2, lambda_k2=lambda_k2, attn_subln=attn_subln, ssm_conv_w=ssm_conv_w,
             ssm_conv_b=ssm_conv_b, ssm_dt_bias=ssm_dt_bias, ssm_a_log=ssm_a_log, ssm_d=ssm_d,
             ssm_norm=ssm_norm, w_out_even=w_out_even, w_in_odd=w_in_odd, gla_w_gate2=gla_w_gate2,
             gla_b_gate=gla_b_gate, gla_norm=gla_norm, w_out_odd=w_out_odd, w_up=w_up,
             ffn_conv_w=ffn_conv_w, ffn_conv_b=ffn_conv_b, w_down=w_down)
    b, seq, d = x_prompt.shape
    w = _prepare_weights(p, d)
    n_meta = meta_tokens.shape[0]
    lp = n_meta + seq
    h0 = jnp.concatenate([jnp.broadcast_to(meta_tokens[None], (b, n_meta, d)), x_prompt], axis=1).reshape(b * lp, d)
    hp, p_k, p_v, p_ssm, p_ssm_conv, p_gla, p_ffn_conv = _run_trunk(
        h0, lp, jnp.arange(lp), None, None, None, None, None, None, p, w)
    y_prompt = hp.reshape(b, lp, d)[:, n_meta:]
    bs, ls, _ = x_sample.shape
    pos_s = cache_k.shape[2] + jnp.arange(ls)
    cache_k = cache_k.reshape(cache_k.shape[:3] + (-1,))
    cache_v = cache_v.reshape(cache_v.shape[:3] + (-1,))
    ys, s_k, s_v, s_ssm, s_ssm_conv, s_gla, s_ffn_conv = _run_trunk(
        x_sample.reshape(bs * ls, d), ls, pos_s, cache_k, cache_v, state_ssm, state_ssm_conv, state_gla,
        state_ffn_conv, p, w)
    return (y_prompt, ys.reshape(bs, ls, d), p_k, p_v, p_ssm, p_ssm_conv, p_gla, p_ffn_conv,
            s_k, s_v, s_ssm, s_ssm_conv, s_gla, s_ffn_conv)
```

```python
import functools
import math

import jax
import jax.numpy as jnp
from jax import lax
from jax.experimental import pallas as pl
from jax.experimental.pallas import tpu as pltpu

F32 = jnp.float32
BF16 = jnp.bfloat16

CHUNK = 64
N_META = 16
EPS = 1e-6
A_HD = 64
ROT_DIM = A_HD // 4
ROPE_THETA = 500000.0
S_HD = 64
S_GROUPS = 4
S_N = 128
S_CONV = 4
G_DK = 256
G_DV = 512
G_RANK = 16
GATE_TAU = 16.0
F_CONV = 3

LANE = 128
SUBLANE = 8
BF16_ROWS = 16
MXU_WIDTH = 256
V7X_VMEM_BYTES = 64 * 1024 * 1024
VMEM_CAP = V7X_VMEM_BYTES - 8 * 1024 * 1024

MM_VMEM_BUDGET = 44 * 1024 * 1024

FF_TILE = 1024
FF_SUB = 256
ATTN_QBLOCK = 512
SSD_CHUNK = 128
GLA_CHUNK = 32
GLA_BLOCK = 256


def _params(semantics, vmem_estimate):
    limit = int(min(VMEM_CAP, max(32 * 1024 * 1024, vmem_estimate * 3 // 2)))
    return pltpu.CompilerParams(dimension_semantics=semantics, vmem_limit_bytes=limit)


def _divisor_tile(n, limit, mult):
    best = None
    for t in range(mult, min(n, limit) + 1, mult):
        if n % t == 0:
            best = t
    assert best is not None, (n, limit, mult)
    return best


def _rms(x, g):
    return x * lax.rsqrt(jnp.mean(x * x, axis=-1, keepdims=True) + EPS) * g


def _split3(x):
    hi = x.astype(BF16)
    r1 = x - hi.astype(F32)
    mid = r1.astype(BF16)
    lo = (r1 - mid.astype(F32)).astype(BF16)
    return hi, mid, lo


def _dot(a, b):
    return jnp.dot(a, b, preferred_element_type=F32)


def _dot_nt(a, b):
    return lax.dot_general(a, b, (((1,), (1,)), ((), ())), preferred_element_type=F32)


def _dot_tn(a, b):
    return lax.dot_general(a, b, (((0,), (0,)), ((), ())), preferred_element_type=F32)


def _prenorm_kernel(h_ref, g_ref, o_ref):
    o_ref[...] = _rms(h_ref[...], g_ref[...]).astype(o_ref.dtype)


def prenorm(h, g):
    rows, d = h.shape
    tr = _divisor_tile(rows, 256, BF16_ROWS)
    est = 2 * tr * d * (4 + 2)
    return pl.pallas_call(
        _prenorm_kernel,
        out_shape=jax.ShapeDtypeStruct((rows, d), BF16),
        grid=(rows // tr,),
        in_specs=[pl.BlockSpec((tr, d), lambda i: (i, 0)), pl.BlockSpec((1, d), lambda i: (0, 0))],
        out_specs=pl.BlockSpec((tr, d), lambda i: (i, 0)),
        compiler_params=_params(("parallel",), est),
        name="prenorm",
    )(h, g.reshape(1, d))


def _resnorm_kernel(h_ref, u_ref, gpost_ref, gpre_ref, hnew_ref, hn_ref):
    hnew = h_ref[...] + _rms(u_ref[...], gpost_ref[...])
    hnew_ref[...] = hnew
    hn_ref[...] = _rms(hnew, gpre_ref[...]).astype(hn_ref.dtype)


def _resnorm_last_kernel(h_ref, u_ref, gpost_ref, hnew_ref):
    hnew_ref[...] = h_ref[...] + _rms(u_ref[...], gpost_ref[...])


def residual_norm(h, u, g_post, g_pre):
    rows, d = h.shape
    tr = _divisor_tile(rows, 256, BF16_ROWS)
    row_spec = pl.BlockSpec((tr, d), lambda i: (i, 0))
    g_spec = pl.BlockSpec((1, d), lambda i: (0, 0))
    if g_pre is None:
        return pl.pallas_call(
            _resnorm_last_kernel,
            out_shape=jax.ShapeDtypeStruct((rows, d), F32),
            grid=(rows // tr,),
            in_specs=[row_spec, row_spec, g_spec],
            out_specs=row_spec,
            compiler_params=_params(("parallel",), 2 * tr * d * 12),
            name="resnorm_last",
        )(h, u, g_post.reshape(1, d)), None
    return pl.pallas_call(
        _resnorm_kernel,
        out_shape=(jax.ShapeDtypeStruct((rows, d), F32), jax.ShapeDtypeStruct((rows, d), BF16)),
        grid=(rows // tr,),
        in_specs=[row_spec, row_spec, g_spec, g_spec],
        out_specs=(row_spec, row_spec),
        compiler_params=_params(("parallel",), 2 * tr * d * 14),
        name="resnorm",
    )(h, u, g_post.reshape(1, d), g_pre.reshape(1, d))


def _mm_kernel(*refs, nx):
    o_ref = refs[2 * nx]
    acc = _dot(refs[0][...], refs[nx][...])
    for t in range(1, nx):
        acc = acc + _dot(refs[t][...], refs[nx + t][...])
    o_ref[...] = acc.astype(o_ref.dtype)


def _mm_tiles(m, k, n, out_bytes):
    def search(w_bufs):
        best = None
        col_step = MXU_WIDTH if n % MXU_WIDTH == 0 else LANE
        for bn in range(col_step, min(n, 4096) + 1, col_step):
            if n % bn:
                continue
            for bm in range(BF16_ROWS, min(m, 2048) + 1, BF16_ROWS):
                if m % bm:
                    continue
                vm = 2 * bm * k * 2 + w_bufs * k * bn * 2 + 2 * bm * bn * out_bytes + bm * bn * 4
                if vm > MM_VMEM_BUDGET:
                    continue
                score = bm * bn / (bm + bn)
                if best is None or score > best[0]:
                    best = (score, bm, bn, vm, w_bufs)
        return best

    single, double = search(1), search(2)
    assert single is not None, (m, k, n)
    best = double if double is not None and double[0] >= 0.9 * single[0] else single
    return best[1:]


def matmul(xs, w, layer, n_cols=None, out_dtype=F32):
    m, kp = xs[0].shape
    assert all(x.shape == (m, kp) for x in xs)
    n = w.shape[2] if n_cols is None else n_cols
    bm, bn, vm, w_bufs = _mm_tiles(m, kp * len(xs), n, jnp.dtype(out_dtype).itemsize)

    def w_spec(t):
        return pl.BlockSpec((None, kp, bn), lambda j, i: (layer, t, j), pipeline_mode=pl.Buffered(w_bufs))

    return pl.pallas_call(
        functools.partial(_mm_kernel, nx=len(xs)),
        out_shape=jax.ShapeDtypeStruct((m, n), out_dtype),
        grid=(n // bn, m // bm),
        in_specs=[pl.BlockSpec((bm, kp), lambda j, i: (i, 0)) for _ in xs] + [w_spec(t) for t in range(len(xs))],
        out_specs=pl.BlockSpec((bm, bn), lambda j, i: (i, j)),
        compiler_params=_params(("parallel", "parallel"), vm),
        name="matmul",
    )(*xs, *([w] * len(xs)))


def _cast_pad_cols_kernel(x_ref, o_ref):
    valid = x_ref.shape[1]
    o_ref[:, :valid] = x_ref[...].astype(o_ref.dtype)
    if o_ref.shape[1] > valid:
        o_ref[:, valid:] = jnp.zeros((o_ref.shape[0], o_ref.shape[1] - valid), o_ref.dtype)


def cast_pad_halves(w, fp):
    layers, d, f2 = w.shape
    f = f2 // 2
    assert f % LANE == 0 and fp % LANE == 0
    tr = _divisor_tile(d, 128, BF16_ROWS)
    return pl.pallas_call(
        _cast_pad_cols_kernel,
        out_shape=jax.ShapeDtypeStruct((layers, d, 2 * fp), BF16),
        grid=(layers, 2, d // tr),
        in_specs=[pl.BlockSpec((None, tr, f), lambda l, s, r: (l, r, s))],
        out_specs=pl.BlockSpec((None, tr, fp), lambda l, s, r: (l, r, s)),
        compiler_params=_params(("parallel", "parallel", "parallel"), 2 * tr * (f * 4 + fp * 2)),
        name="cast_pad_halves",
    )(w)


def _cast_pad_rows_kernel(x_ref, o_ref, *, n_valid):
    r = pl.program_id(1)

    @pl.when(r < n_valid)
    def _():
        o_ref[...] = x_ref[...].astype(o_ref.dtype)

    @pl.when(r >= n_valid)
    def _():
        o_ref[...] = jnp.zeros(o_ref.shape, o_ref.dtype)


def cast_pad_rows(w, fp):
    layers, f, d = w.shape
    tr = _divisor_tile(math.gcd(f, fp), 256, BF16_ROWS)
    n_valid = f // tr
    return pl.pallas_call(
        functools.partial(_cast_pad_rows_kernel, n_valid=n_valid),
        out_shape=jax.ShapeDtypeStruct((layers, fp, d), BF16),
        grid=(layers, fp // tr),
        in_specs=[pl.BlockSpec((None, tr, d), lambda l, r: (l, jnp.minimum(r, n_valid - 1), 0))],
        out_specs=pl.BlockSpec((None, tr, d), lambda l, r: (l, r, 0)),
        compiler_params=_params(("parallel", "parallel"), 2 * tr * d * 6),
        name="cast_pad_rows",
    )(w)


def _ffn_up_kernel(x_ref, wg_ref, wv_ref, cw_ref, cb_ref, prev_ref, a_ref, st_ref, carry_ref,
                   *, nseq, rows, nblk):
    x = x_ref[...]
    tf = wg_ref.shape[1]
    if nblk > 1:
        first = (pl.program_id(1) % nblk) == 0

        @pl.when(first)
        def _():
            carry_ref[...] = jnp.zeros(carry_ref.shape, F32)

    t = lax.broadcasted_iota(jnp.int32, (1, rows, 1), 1)
    for c0 in range(0, tf, FF_SUB):
        cs = slice(c0, c0 + FF_SUB)
        g = _dot(x, wg_ref[:, cs])
        v = _dot(x, wv_ref[:, cs])
        if nblk == 1:
            prev = prev_ref[:, :, cs]
        else:
            prev = jnp.where(first, prev_ref[:, :, cs], carry_ref[SUBLANE - 2:SUBLANE, cs][None])
            carry_ref[:, cs] = g[rows - SUBLANE:rows, :]
        g3 = g.reshape(nseq, rows, FF_SUB)
        g1 = pltpu.roll(g, 1, axis=0).reshape(nseq, rows, FF_SUB)
        g2 = pltpu.roll(g, 2, axis=0).reshape(nseq, rows, FF_SUB)
        p0 = prev[:, 0:1, :]
        p1 = prev[:, 1:2, :]
        g1 = jnp.where(t == 0, p1, g1)
        g2 = jnp.where(t == 0, p0, jnp.where(t == 1, p1, g2))
        cw = cw_ref[:, cs]
        c = cw[0:1, :] * g2 + cw[1:2, :] * g1 + cw[2:3, :] * g3 + cb_ref[:, cs]
        a = jax.nn.gelu(c) * v.reshape(nseq, rows, FF_SUB)
        a_ref[:, cs] = a.reshape(nseq * rows, FF_SUB).astype(a_ref.dtype)
        st_ref[:, :, cs] = g3[:, rows - 2:rows, :]


def ffn_up(hn, w_up, conv_w, conv_b, layer, prev, seq_len):
    m, d = hn.shape
    f = w_up.shape[2] // 2
    nb = m // seq_len
    tf = FF_TILE
    nt = f // tf
    if seq_len >= 512:
        rows = _divisor_tile(seq_len, 1024, BF16_ROWS)
        nseq, nblk = 1, seq_len // rows
    else:
        rows, nblk = seq_len, 1
        nseq = _divisor_tile(nb, max(1, 1024 // seq_len), 1)
    bm = nseq * rows
    est = 2 * bm * d * 2 + 2 * d * tf * 2 + 2 * bm * tf * 2 + 10 * bm * FF_SUB * 4
    resident = dict(pipeline_mode=pl.Buffered(1))
    a, st = pl.pallas_call(
        functools.partial(_ffn_up_kernel, nseq=nseq, rows=rows, nblk=nblk),
        out_shape=(jax.ShapeDtypeStruct((m, f), BF16), jax.ShapeDtypeStruct((nb, F_CONV - 1, f), F32)),
        grid=(nt, m // bm),
        in_specs=[pl.BlockSpec((bm, d), lambda j, i: (i, 0)),
                  pl.BlockSpec((None, d, tf), lambda j, i: (layer, 0, j), **resident),
                  pl.BlockSpec((None, d, tf), lambda j, i: (layer, 0, nt + j), **resident),
                  pl.BlockSpec((None, F_CONV, tf), lambda j, i: (layer, 0, j)),
                  pl.BlockSpec((None, 1, tf), lambda j, i: (layer, 0, j)),
                  pl.BlockSpec((nseq, F_CONV - 1, tf), lambda j, i: (i // nblk, 0, j))],
        out_specs=(pl.BlockSpec((bm, tf), lambda j, i: (i, j)),
                   pl.BlockSpec((nseq, F_CONV - 1, tf), lambda j, i: (i // nblk, 0, j))),
        scratch_shapes=[pltpu.VMEM((SUBLANE, tf), F32)],
        compiler_params=_params(("parallel", "arbitrary"), est),
        name="ffn_up",
    )(hn, w_up, w_up, conv_w, conv_b, prev)
    return a, st


def _rope_tables(pos):
    half = ROT_DIM // 2
    inv = jnp.exp(-math.log(ROPE_THETA) * jnp.arange(half, dtype=F32) * 2.0 / ROT_DIM)
    ang = pos.astype(F32)[:, None] * inv[None, :]
    cos, sin = jnp.cos(ang), jnp.sin(ang)
    n = pos.shape[0]
    one = jnp.ones((n, A_HD - ROT_DIM), F32)
    zero = jnp.zeros((n, A_HD - ROT_DIM), F32)
    zh = jnp.zeros((n, half), F32)
    c = jnp.concatenate([cos, cos, one], axis=1)
    s_up = jnp.concatenate([-sin, zh, zero], axis=1)
    s_dn = jnp.concatenate([zh, sin, zero], axis=1)
    return tuple(jnp.concatenate([t, t], axis=1) for t in (c, s_up, s_dn))


def _rope(x, c, s_up, s_dn):
    half = ROT_DIM // 2
    return (x * c + pltpu.roll(x, 2 * A_HD - half, axis=1) * s_up + pltpu.roll(x, half, axis=1) * s_dn)


def _lambda(lq1, lk1, lq2, lk2, lambda_init):
    l1 = jnp.sum(lq1[...] * lk1[...], axis=-1, keepdims=True)
    l2 = jnp.sum(lq2[...] * lk2[...], axis=-1, keepdims=True)
    return jnp.exp(l1) - jnp.exp(l2) + lambda_init


def _two_softmax_pv(q, parts, lam):
    lane = lax.broadcasted_iota(jnp.int32, q.shape, 1)
    outs = []
    for sel in (lane < A_HD, lane >= A_HD):
        qs = jnp.where(sel, q, 0.0).astype(BF16)
        scores = []
        for kb, _, mask in parts:
            s = _dot_nt(qs, kb)
            if mask is not None:
                s = jnp.where(mask, s, -jnp.inf)
            scores.append(s)
        mx = scores[0].max(axis=-1, keepdims=True)
        for s in scores[1:]:
            mx = jnp.maximum(mx, s.max(axis=-1, keepdims=True))
        den = None
        num = None
        for s, (_, vb, _) in zip(scores, parts):
            p = jnp.exp(s - mx)
            d = jnp.sum(p, axis=-1, keepdims=True)
            o = _dot(p.astype(BF16), vb)
            den = d if den is None else den + d
            num = o if num is None else num + o
        outs.append(num / den)
    return outs[0] - lam * outs[1]


def _attn_prompt_kernel(q_ref, k_ref, v_ref, c_ref, su_ref, sd_ref, lq1, lk1, lq2, lk2, sub_ref,
                        o_ref, kr_ref, kb_ref, vb_ref, *, seq_len, pad_len, lambda_init):
    lam = _lambda(lq1, lk1, lq2, lk2, lambda_init)
    c, su, sd = c_ref[...], su_ref[...], sd_ref[...]
    k = _rope(k_ref[...], c, su, sd)
    kr_ref[...] = k
    kb_ref[0:seq_len, :] = k.astype(BF16)
    vb_ref[0:seq_len, :] = v_ref[...].astype(BF16)
    if pad_len > seq_len:
        kb_ref[seq_len:pad_len, :] = jnp.zeros((pad_len - seq_len, 2 * A_HD), BF16)
        vb_ref[seq_len:pad_len, :] = jnp.zeros((pad_len - seq_len, 2 * A_HD), BF16)
    qb = ATTN_QBLOCK
    shift = CHUNK - N_META
    for r0 in range(0, seq_len, qb):
        r1 = min(seq_len, r0 + qb)
        last_chunk = (r1 - 1 + shift) // CHUNK
        ke = min(pad_len, -(-(N_META + CHUNK * last_chunk) // LANE) * LANE)
        q = _rope(q_ref[r0:r1, :], c[r0:r1], su[r0:r1], sd[r0:r1]) * (A_HD ** -0.5)
        cq = (lax.broadcasted_iota(jnp.int32, (r1 - r0, 1), 0) + (r0 + shift)) // CHUNK
        ck = (lax.broadcasted_iota(jnp.int32, (1, ke), 1) + shift) // CHUNK
        o = _two_softmax_pv(q, [(kb_ref[0:ke, :], vb_ref[0:ke, :], cq >= ck)], lam)
        o_ref[r0:r1, :] = (_rms(o, sub_ref[...]) * (1.0 - lambda_init)).astype(o_ref.dtype)


def _attn_sample_kernel(q_ref, k_ref, v_ref, kp_ref, vp_ref, c_ref, su_ref, sd_ref, lq1, lk1, lq2, lk2,
                        sub_ref, o_ref, kr_ref, *, lambda_init):
    lam = _lambda(lq1, lk1, lq2, lk2, lambda_init)
    c, su, sd = c_ref[...], su_ref[...], sd_ref[...]
    k = _rope(k_ref[...], c, su, sd)
    kr_ref[...] = k
    q = _rope(q_ref[...], c, su, sd) * (A_HD ** -0.5)
    parts = [(kp_ref[0].astype(BF16), vp_ref[0].astype(BF16), None),
             (k.astype(BF16), v_ref[...].astype(BF16), None)]
    o = _two_softmax_pv(q, parts, lam)
    o_ref[...] = (_rms(o, sub_ref[...]) * (1.0 - lambda_init)).astype(o_ref.dtype)


def diff_attention(proj, pos, seq_len, n_heads, lams, subln, lambda_init, k_past=None, v_past=None, layer=0):
    m = proj.shape[0]
    nb = m // seq_len
    hw = 2 * A_HD
    tabs = _rope_tables(pos)
    vec = lambda: pl.BlockSpec((1, A_HD), lambda b, h: (0, 0))
    tab = lambda: pl.BlockSpec((seq_len, hw), lambda b, h: (0, 0))
    col = lambda off: pl.BlockSpec((seq_len, hw), lambda b, h: (b, off + h))
    out_specs = (pl.BlockSpec((seq_len, hw), lambda b, h: (b, h)),
                 pl.BlockSpec((seq_len, hw), lambda b, h: (b, h)))
    out_shape = (jax.ShapeDtypeStruct((m, n_heads * hw), BF16), jax.ShapeDtypeStruct((m, n_heads * hw), F32))
    lam_args = [l.reshape(1, A_HD) for l in lams]
    sub = subln.reshape(1, hw)
    sub_spec = pl.BlockSpec((1, hw), lambda b, h: (0, 0))
    if k_past is None:
        pad_len = -(-seq_len // LANE) * LANE
        est = 2 * seq_len * hw * (3 * 4 + 3 * 4 + 2 + 4) + 2 * pad_len * hw * 2 + 8 * LANE * pad_len * 4
        return pl.pallas_call(
            functools.partial(_attn_prompt_kernel, seq_len=seq_len, pad_len=pad_len, lambda_init=lambda_init),
            out_shape=out_shape,
            grid=(nb, n_heads),
            in_specs=[col(0), col(n_heads), col(2 * n_heads), tab(), tab(), tab(),
                      vec(), vec(), vec(), vec(), sub_spec],
            out_specs=out_specs,
            scratch_shapes=[pltpu.VMEM((pad_len, hw), BF16), pltpu.VMEM((pad_len, hw), BF16)],
            compiler_params=_params(("parallel", "parallel"), est),
            name="attn_prompt",
        )(proj, proj, proj, *tabs, *lam_args, sub)
    past = k_past.shape[2]
    kp, vp = k_past, v_past
    past_spec = lambda: pl.BlockSpec((None, 1, past, hw), lambda b, h: (layer, b, 0, h))
    est = 2 * (2 * past * hw * 4 + 8 * seq_len * hw * 4) + 8 * seq_len * past * 4
    return pl.pallas_call(
        functools.partial(_attn_sample_kernel, lambda_init=lambda_init),
        out_shape=out_shape,
        grid=(nb, n_heads),
        in_specs=[col(0), col(n_heads), col(2 * n_heads), past_spec(), past_spec(), tab(), tab(), tab(),
                  vec(), vec(), vec(), vec(), sub_spec],
        out_specs=out_specs,
        compiler_params=_params(("parallel", "parallel"), est),
        name="attn_sample",
    )(proj, proj, proj, kp, vp, *tabs, *lam_args, sub)


def _cumsum_rows(x):
    n = x.shape[0]
    row = lax.broadcasted_iota(jnp.int32, (n, 1), 0)
    sh = 1
    while sh < n:
        x = x + jnp.where(row >= sh, pltpu.roll(x, sh, axis=0), 0.0)
        sh *= 2
    return x


def _conv4_silu(ext, w, b, rows):
    acc = b + w[S_CONV - 1:S_CONV, :] * ext[SUBLANE:SUBLANE + rows, :]
    for j in range(S_CONV - 1):
        back = S_CONV - 1 - j
        acc = acc + w[j:j + 1, :] * pltpu.roll(ext, back, axis=0)[SUBLANE:SUBLANE + rows, :]
    return jax.nn.silu(acc)


def _ssd_kernel(xs_ref, bm_ref, cm_ref, z_ref, dt_ref, cwx_ref, cwb_ref, cwc_ref, cbx_ref, cbb_ref, cbc_ref,
                c0x_ref, c0b_ref, c0c_ref, dtb_ref, alog_ref, dsk_ref, nrm_ref, st0_ref,
                y_ref, stout_ref, s_ref, *, seq_len, chunk, has_state):
    hp = xs_ref.shape[1]
    nh = hp // S_HD
    if has_state:
        s_ref[...] = st0_ref[0].reshape(hp, S_N).T
    else:
        s_ref[...] = jnp.zeros((S_N, hp), F32)
    a_neg = -jnp.exp(alog_ref[...])
    dtb = dtb_ref[...]
    e_row = lax.broadcasted_iota(jnp.int32, (LANE, hp), 0)
    e_col = lax.broadcasted_iota(jnp.int32, (LANE, hp), 1) // S_HD
    expand = (e_row == e_col).astype(BF16)
    eye = (lax.broadcasted_iota(jnp.int32, (LANE, LANE), 0)
           == lax.broadcasted_iota(jnp.int32, (LANE, LANE), 1)).astype(BF16)
    lane_hp = lax.broadcasted_iota(jnp.int32, (1, 2 * S_HD), 1)

    def pad8(c0_ref):
        c0 = c0_ref[0]
        return jnp.concatenate([jnp.zeros((SUBLANE - (S_CONV - 1), c0.shape[1]), F32), c0], axis=0)

    def ext_of(ref, start, rows, c0_ref):
        if isinstance(start, int) and start == 0:
            return jnp.concatenate([pad8(c0_ref), ref[0:rows, :]], axis=0)
        return ref[pl.ds(start - SUBLANE, rows + SUBLANE), :]

    def to_heads(v):
        hi, mid, lo = _split3(v)
        return _dot(hi, expand) + _dot(mid, expand) + _dot(lo, expand)

    def step(start, rows):
        x = _conv4_silu(ext_of(xs_ref, start, rows, c0x_ref), cwx_ref[...], cbx_ref[...], rows)
        bmat = _conv4_silu(ext_of(bm_ref, start, rows, c0b_ref), cwb_ref[...], cbb_ref[...], rows)
        cmat = _conv4_silu(ext_of(cm_ref, start, rows, c0c_ref), cwc_ref[...], cbc_ref[...], rows)
        dt = jax.nn.softplus(dt_ref[pl.ds(start, rows), :] + dtb)
        cum = _cumsum_rows(dt * a_neg)
        hi, mid, lo = _split3(cum)
        cum_t = _dot_nt(eye, hi) + _dot_nt(eye, mid) + _dot_nt(eye, lo)
        cumx = to_heads(cum)
        xdt = x * to_heads(dt)
        xdt_b = xdt.astype(BF16)
        c_b = cmat.astype(BF16)
        b_b = bmat.astype(BF16)
        cb = _dot_nt(c_b, b_b)
        causal = (lax.broadcasted_iota(jnp.int32, (rows, rows), 0)
                  >= lax.broadcasted_iota(jnp.int32, (rows, rows), 1))
        cols = []
        for pair in range(nh // 2):
            xpair = xdt_b[:, pair * 2 * S_HD:(pair + 1) * 2 * S_HD]
            acc = None
            for sub in range(2):
                h = 2 * pair + sub
                seg = cum[:, h:h + 1] - cum_t[h:h + 1, :]
                wgt = (cb * jnp.exp(jnp.where(causal, seg, -jnp.inf))).astype(BF16)
                keep = (lane_hp < S_HD) if sub == 0 else (lane_hp >= S_HD)
                part = _dot(wgt, jnp.where(keep, xpair, jnp.zeros_like(xpair)))
                acc = part if acc is None else acc + part
            cols.append(acc)
        y = jnp.concatenate(cols, axis=1)
        s_old = s_ref[...]
        y = y + _dot(c_b, s_old.astype(BF16)) * jnp.exp(cumx)
        y = y + x * dsk_ref[...]
        last = cumx[rows - 1:rows, :]
        tail = jnp.exp(last - cumx)
        s_ref[...] = s_old * jnp.exp(last) + _dot_tn(b_b, (xdt * tail).astype(BF16))
        yz = y * jax.nn.silu(z_ref[pl.ds(start, rows), :])
        y_ref[pl.ds(start, rows), :] = _rms(yz, nrm_ref[...]).astype(y_ref.dtype)

    head = seq_len % chunk
    if head:
        step(0, head)
    n_full = seq_len // chunk
    if n_full == 1 and head == 0:
        step(0, chunk)
    elif n_full:
        def body(i, carry):
            step(pl.multiple_of(head + i * chunk, BF16_ROWS), chunk)
            return carry
        if head == 0:
            step(0, chunk)
            lax.fori_loop(1, n_full, body, 0)
        else:
            lax.fori_loop(0, n_full, body, 0)
    stout_ref[0] = s_ref[...].T.reshape(nh, S_HD, S_N)


def ssd_mixer(proj, dtp, x_off, z_off, seq_len, conv_w, conv_b, conv0, dt_bias, a_log, d_skip, s_norm, state0):
    m = proj.shape[0]
    nb = m // seq_len
    heads = a_log.shape[0]
    inner = heads * S_HD
    g = (conv_w.shape[1] - inner) // (2 * S_N)
    nh = heads // g
    hp = nh * S_HD
    chunk = min(SSD_CHUNK, seq_len)

    def per_head(v):
        return jnp.zeros((g, LANE), F32).at[:, :nh].set(v.reshape(g, nh)).reshape(1, g * LANE)

    xb, bb, cb = x_off // hp, (x_off + inner) // S_N, (x_off + inner + g * S_N) // S_N
    row = lambda width, blk: pl.BlockSpec((seq_len, width), lambda b, j: (b, blk + j))
    wrow = lambda rows, width, blk: pl.BlockSpec((rows, width), lambda b, j: (0, blk + j))
    c0 = lambda width, blk: pl.BlockSpec((1, S_CONV - 1, width), lambda b, j: (b, 0, blk + j))
    has_state = state0 is not None
    if not has_state:
        state0 = jnp.zeros((1, heads, S_HD, S_N), F32)
        st_spec = pl.BlockSpec((1, nh, S_HD, S_N), lambda b, j: (0, j, 0, 0))
    else:
        st_spec = pl.BlockSpec((1, nh, S_HD, S_N), lambda b, j: (b, j, 0, 0))
    cbias = conv_b.reshape(1, -1)
    ib, icb, icc = 0, inner // S_N, (inner + g * S_N) // S_N
    est = 2 * seq_len * (2 * hp * 4 + 3 * LANE * 4 + hp * 2) + 40 * chunk * hp * 4
    return pl.pallas_call(
        functools.partial(_ssd_kernel, seq_len=seq_len, chunk=chunk, has_state=has_state),
        out_shape=(jax.ShapeDtypeStruct((m, inner), BF16), jax.ShapeDtypeStruct((nb, heads, S_HD, S_N), F32)),
        grid=(nb, g),
        in_specs=[row(hp, xb), row(S_N, bb), row(S_N, cb), row(hp, z_off // hp),
                  pl.BlockSpec((seq_len, LANE), lambda b, j: (b, j)),
                  wrow(S_CONV, hp, ib), wrow(S_CONV, S_N, icb), wrow(S_CONV, S_N, icc),
                  wrow(1, hp, ib), wrow(1, S_N, icb), wrow(1, S_N, icc),
                  c0(hp, ib), c0(S_N, icb), c0(S_N, icc),
                  wrow(1, LANE, 0), wrow(1, LANE, 0), wrow(1, hp, 0), wrow(1, hp, 0), st_spec],
        out_specs=(pl.BlockSpec((seq_len, hp), lambda b, j: (b, j)),
                   pl.BlockSpec((1, nh, S_HD, S_N), lambda b, j: (b, j, 0, 0))),
        scratch_shapes=[pltpu.VMEM((S_N, hp), F32)],
        compiler_params=_params(("parallel", "parallel"), est),
        name="ssd",
    )(proj, proj, proj, proj, dtp, conv_w, conv_w, conv_w, cbias, cbias, cbias, conv0, conv0, conv0,
      per_head(dt_bias), per_head(a_log), jnp.repeat(d_skip, S_HD).reshape(1, inner), s_norm.reshape(1, inner),
      state0)


def _gla_kernel(q_ref, k_ref, v_ref, r_ref, gl_ref, w2_ref, bg_ref, gn_ref, s0_ref, o_ref, sout_ref,
                s_ref, cum_ref, qi_ref, kt_ref, of_ref, *, seq_len, chunk, has_state):
    if has_state:
        s_ref[...] = s0_ref[0, 0].T
    else:
        s_ref[...] = jnp.zeros(s_ref.shape, F32)
    scale = G_DK ** -0.5
    n_full = seq_len // chunk
    tail = seq_len - n_full * chunk
    assert tail & (tail - 1) == 0 and chunk & (chunk - 1) == 0

    def prep(start, rows, c):
        sl = pl.ds(start, rows)
        zg = _dot(gl_ref[sl, :].astype(BF16), w2_ref[...]) + bg_ref[...]
        cum = jax.nn.log_sigmoid(zg) / GATE_TAU
        pos = lax.broadcasted_iota(jnp.int32, (rows, 1), 0) & (c - 1)
        sh = 1
        while sh < c:
            cum = cum + jnp.where(pos >= sh, pltpu.roll(cum, sh, axis=0), 0.0)
            sh *= 2
        cum_ref[sl, :] = cum
        n = rows // c
        cum3 = cum.reshape(n, c, G_DK)
        anchor = cum3[:, c // 2 - 1:c // 2, :]
        last = cum3[:, c - 1:c, :]
        q3 = (q_ref[sl, :] * scale).reshape(n, c, G_DK)
        k3 = k_ref[sl, :].reshape(n, c, G_DK)
        flat = lambda a: a.reshape(rows, G_DK).astype(BF16)
        qi_ref[sl, :] = flat(q3 * jnp.exp(cum3))
        kt_ref[sl, :] = flat(k3 * jnp.exp(last - cum3))
        qa = (q3 * jnp.exp(cum3 - anchor)).astype(BF16)
        ka = (k3 * jnp.exp(anchor - cum3)).astype(BF16)
        att = jnp.einsum("ctd,csd->cts", qa, ka, preferred_element_type=F32)
        causal = (lax.broadcasted_iota(jnp.int32, (1, c, c), 1) >= lax.broadcasted_iota(jnp.int32, (1, c, c), 2))
        att = jnp.where(causal, att, 0.0).astype(BF16)
        v3 = v_ref[sl, :].astype(BF16).reshape(n, c, G_DV)
        of_ref[sl, :] = jnp.einsum("cts,cse->cte", att, v3, preferred_element_type=F32).reshape(rows, G_DV)

    def post(start, rows):
        sl = pl.ds(start, rows)
        o_ref[sl, :] = (_rms(of_ref[sl, :], gn_ref[...]) * jax.nn.silu(r_ref[sl, :])).astype(o_ref.dtype)

    def blocks(fn, *extra):
        main = n_full * chunk
        nblk = main // GLA_BLOCK
        if nblk:
            def body(i, carry):
                fn(pl.multiple_of(i * GLA_BLOCK, GLA_BLOCK), GLA_BLOCK, *extra)
                return carry
            lax.fori_loop(0, nblk, body, 0)
        if main > nblk * GLA_BLOCK:
            fn(nblk * GLA_BLOCK, main - nblk * GLA_BLOCK, *extra)

    blocks(prep, chunk)
    if tail:
        prep(n_full * chunk, tail, tail)

    def step(start, c):
        sl = pl.ds(start, c)
        s_old = s_ref[...]
        of_ref[sl, :] += _dot_nt(qi_ref[sl, :], s_old.astype(BF16))
        last = cum_ref[pl.ds(start + c - SUBLANE, SUBLANE), :][SUBLANE - 1:SUBLANE, :]
        s_ref[...] = s_old * jnp.exp(last) + _dot_tn(v_ref[sl, :].astype(BF16), kt_ref[sl, :])

    def body(i, carry):
        step(pl.multiple_of(i * chunk, chunk), chunk)
        return carry
    lax.fori_loop(0, n_full, body, 0, unroll=4 if n_full % 4 == 0 else 1)
    if tail:
        step(n_full * chunk, tail)

    blocks(post)
    if tail:
        post(n_full * chunk, tail)
    sout_ref[0, 0] = s_ref[...].T


def gla_mixer(proj, glow, seq_len, n_heads, w_gate2, b_gate, g_norm, state0):
    m = proj.shape[0]
    nb = m // seq_len
    kq, vq = n_heads * G_DK, n_heads * G_DV
    has_state = state0 is not None
    if not has_state:
        state0 = jnp.zeros((1, 1, G_DK, G_DV), F32)
        st_spec = pl.BlockSpec((1, 1, G_DK, G_DV), lambda b, h: (0, 0, 0, 0))
    else:
        st_spec = pl.BlockSpec((1, 1, G_DK, G_DV), lambda b, h: (b, h, 0, 0))
    w2 = jnp.zeros((LANE, kq), BF16).at[:G_RANK].set(w_gate2.astype(BF16))
    row = lambda width, blk: pl.BlockSpec((seq_len, width), lambda b, h: (b, blk + h))
    est = (2 * seq_len * (2 * G_DK * 4 + 2 * G_DV * 4 + LANE * 4 + G_DV * 2)
           + seq_len * (G_DK * 4 + 4 * G_DK * 2 + G_DV * 4) + 8 * G_DK * G_DV * 4)
    return pl.pallas_call(
        functools.partial(_gla_kernel, seq_len=seq_len, chunk=GLA_CHUNK, has_state=has_state),
        out_shape=(jax.ShapeDtypeStruct((m, vq), BF16), jax.ShapeDtypeStruct((nb, n_heads, G_DK, G_DV), F32)),
        grid=(nb, n_heads),
        in_specs=[row(G_DK, 0), row(G_DK, kq // G_DK), row(G_DV, 2 * kq // G_DV), row(G_DV, (2 * kq + vq) // G_DV),
                  pl.BlockSpec((seq_len, LANE), lambda b, h: (b, 0)),
                  pl.BlockSpec((LANE, G_DK), lambda b, h: (0, h)),
                  pl.BlockSpec((1, G_DK), lambda b, h: (0, h)),
                  pl.BlockSpec((1, G_DV), lambda b, h: (0, 0)),
                  st_spec],
        out_specs=(pl.BlockSpec((seq_len, G_DV), lambda b, h: (b, h)),
                   pl.BlockSpec((1, 1, G_DK, G_DV), lambda b, h: (b, h, 0, 0))),
        scratch_shapes=[pltpu.VMEM((G_DV, G_DK), F32), pltpu.VMEM((seq_len, G_DK), F32),
                        pltpu.VMEM((seq_len, G_DK), BF16), pltpu.VMEM((seq_len, G_DK), BF16),
                        pltpu.VMEM((seq_len, G_DV), F32)],
        compiler_params=_params(("parallel", "parallel"), est),
        name="gla",
    )(proj, proj, proj, proj, glow, w2, b_gate.reshape(1, kq), g_norm.reshape(1, G_DV), state0)


def _prepare_weights(p, d_model):
    depth = p["w_up"].shape[0]
    n_even = p["w_in_even"].shape[0]
    heads_s = p["ssm_a_log"].shape[1]
    inner = heads_s * S_HD
    a_width = d_model // 2
    a_qk = a_width
    conv_dim = p["ssm_conv_w"].shape[2]
    groups = (conv_dim - inner) // (2 * S_N)
    main_e = 2 * a_qk + a_width + inner + conv_dim
    nh = heads_s // groups
    f = p["w_up"].shape[2] // 2
    fp = -(-f // FF_TILE) * FF_TILE
    kq = p["gla_w_gate2"].shape[2]
    main_o = 2 * kq + 2 * d_model
    w = {}
    w["in_even"] = p["w_in_even"].astype(BF16)
    dtw = p["w_in_even"][:, :, main_e:].reshape(n_even, d_model, groups, nh)
    w["dt"] = jnp.pad(dtw, ((0, 0), (0, 0), (0, 0), (0, LANE - nh))).reshape(n_even, d_model, groups * LANE).astype(BF16)
    w["out_even"] = p["w_out_even"].astype(BF16)
    w["in_odd"] = p["w_in_odd"].astype(BF16)
    w["glow"] = jnp.pad(p["w_in_odd"][:, :, main_o:], ((0, 0), (0, 0), (0, LANE - G_RANK))).astype(BF16)
    w["out_odd"] = p["w_out_odd"].astype(BF16)
    w["up"] = cast_pad_halves(p["w_up"], fp)
    w["down"] = cast_pad_rows(p["w_down"], fp)
    w["ffn_conv_w"] = jnp.pad(p["ffn_conv_w"], ((0, 0), (0, 0), (0, fp - f)))
    w["ffn_conv_b"] = jnp.pad(p["ffn_conv_b"], ((0, 0), (0, fp - f))).reshape(depth, 1, fp)
    w["f"], w["fp"], w["main_e"], w["main_o"] = f, fp, main_e, main_o
    return w


def _run_trunk(h, seq_len, pos, k_past, v_past, ssm0, sconv0, gla0, fconv0, p, w):
    rows, d = h.shape
    nb = rows // seq_len
    depth = p["w_up"].shape[0]
    a_heads = d // 2 // (2 * A_HD)
    a_w = a_heads * 2 * A_HD
    heads_s = p["ssm_a_log"].shape[1]
    inner = heads_s * S_HD
    conv_dim = p["ssm_conv_w"].shape[2]
    g_heads = d // G_DV
    f, fp = w["f"], w["fp"]
    ks, vs, ssms, sconvs, glas, fconvs = [], [], [], [], [], []
    hn = prenorm(h, p["norm_pre_mix"][0])
    for i in range(depth):
        if i % 2 == 0:
            e = i // 2
            lambda_init = 0.8 - 0.6 * math.exp(-0.3 * i)
            proj = matmul([hn], w["in_even"], e, n_cols=w["main_e"])
            dtp = matmul([hn], w["dt"], e)
            lams = (p["lambda_q1"][e], p["lambda_k1"][e], p["lambda_q2"][e], p["lambda_k2"][e])
            o_a, k_rot = diff_attention(proj, pos, seq_len, a_heads, lams, p["attn_subln"][e], lambda_init,
                                        k_past, v_past, e)
            conv0 = jnp.zeros((nb, S_CONV - 1, conv_dim), F32) if sconv0 is None else sconv0[e]
            y_s, ssm_new = ssd_mixer(proj, dtp, 3 * a_w + inner, 3 * a_w, seq_len, p["ssm_conv_w"][e],
                                     p["ssm_conv_b"][e], conv0, p["ssm_dt_bias"][e], p["ssm_a_log"][e],
                                     p["ssm_d"][e], p["ssm_norm"][e], None if ssm0 is None else ssm0[e])
            out = matmul([o_a, y_s], w["out_even"], e)
            proj3 = proj.reshape(nb, seq_len, -1)
            ks.append(k_rot.reshape(nb, seq_len, 2 * a_heads, A_HD))
            vs.append(proj3[:, :, 2 * a_w:3 * a_w].reshape(nb, seq_len, a_heads, 2 * A_HD))
            ssms.append(ssm_new)
            sconvs.append(proj3[:, seq_len - (S_CONV - 1):, 3 * a_w + inner:3 * a_w + inner + conv_dim])
        else:
            o = i // 2
            proj = matmul([hn], w["in_odd"], o, n_cols=w["main_o"])
            glow = matmul([hn], w["glow"], o)
            y_g, gla_new = gla_mixer(proj, glow, seq_len, g_heads, p["gla_w_gate2"][o], p["gla_b_gate"][o],
                                     p["gla_norm"][o], None if gla0 is None else gla0[o])
            out = matmul([y_g], w["out_odd"], o)
            glas.append(gla_new)
        h, hn = residual_norm(h, out, p["norm_post_mix"][i], p["norm_pre_ffn"][i])
        prev = jnp.zeros((nb, F_CONV - 1, fp), F32) if fconv0 is None else jnp.pad(fconv0[i], ((0, 0), (0, 0), (0, fp - f)))
        a, fconv_new = ffn_up(hn, w["up"], w["ffn_conv_w"], w["ffn_conv_b"], i, prev, seq_len)
        fconvs.append(fconv_new[:, :, :f])
        ff = matmul([a], w["down"], i)
        h, hn = residual_norm(h, ff, p["norm_post_ffn"][i], p["norm_pre_mix"][i + 1] if i + 1 < depth else None)
    return (h, jnp.stack(ks), jnp.stack(vs), jnp.stack(ssms), jnp.stack(sconvs), jnp.stack(glas), jnp.stack(fconvs))


def kernel(x_prompt, x_sample, cache_k, cache_v, state_ssm, state_ssm_conv, state_gla, state_ffn_conv, meta_tokens, norm_pre_mix, norm_post_mix, norm_pre_ffn, norm_post_ffn, w_in_even, lambda_q1, lambda_k1, lambda_q2, lambda_k2, attn_subln, ssm_conv_w, ssm_conv_b, ssm_dt_bias, ssm_a_log, ssm_d, ssm_norm, w_out_even, w_in_odd, gla_w_gate2, gla_b_gate, gla_norm, w_out_odd, w_up, ffn_conv_w, ffn_conv_b, w_down):
    p = dict(norm_pre_mix=norm_pre_mix, norm_post_mix=norm_post_mix, norm_pre_ffn=norm_pre_ffn,
             norm_post_ffn=norm_post_ffn, w_in_even=w_in_even, lambda_q1=lambda_q1, lambda_k1=lambda_k1,
             lambda_q2=lambda_q2, lambda_k2=lambda_k2, attn_subln=attn_subln, ssm_conv_w=ssm_conv_w,
             ssm_conv_b=ssm_conv_b, ssm_dt_bias=ssm_dt_bias, ssm_a_log=ssm_a_log, ssm_d=ssm_d,
             ssm_norm=ssm_norm, w_out_even=w_out_even, w_in_odd=w_in_odd, gla_w_gate2=gla_w_gate2,
             gla_b_gate=gla_b_gate, gla_norm=gla_norm, w_out_odd=w_out_odd, w_up=w_up,
             ffn_conv_w=ffn_conv_w, ffn_conv_b=ffn_conv_b, w_down=w_down)
    b, seq, d = x_prompt.shape
    w = _prepare_weights(p, d)
    n_meta = meta_tokens.shape[0]
    lp = n_meta + seq
    h0 = jnp.concatenate([jnp.broadcast_to(meta_tokens[None], (b, n_meta, d)), x_prompt], axis=1).reshape(b * lp, d)
    hp, p_k, p_v, p_ssm, p_ssm_conv, p_gla, p_ffn_conv = _run_trunk(
        h0, lp, jnp.arange(lp), None, None, None, None, None, None, p, w)
    y_prompt = hp.reshape(b, lp, d)[:, n_meta:]
    bs, ls, _ = x_sample.shape
    pos_s = cache_k.shape[2] + jnp.arange(ls)
    cache_k = cache_k.reshape(cache_k.shape[:3] + (-1,))
    cache_v = cache_v.reshape(cache_v.shape[:3] + (-1,))
    ys, s_k, s_v, s_ssm, s_ssm_conv, s_gla, s_ffn_conv = _run_trunk(
        x_sample.reshape(bs * ls, d), ls, pos_s, cache_k, cache_v, state_ssm, state_ssm_conv, state_gla,
        state_ffn_conv, p, w)
    return (y_prompt, ys.reshape(bs, ls, d), p_k, p_v, p_ssm, p_ssm_conv, p_gla, p_ffn_conv,
            s_k, s_v, s_ssm, s_ssm_conv, s_gla, s_ffn_conv)
```

```python
import functools
import math

import jax
import jax.numpy as jnp
from jax import lax
from jax.experimental import pallas as pl
from jax.experimental.pallas import tpu as pltpu

F32 = jnp.float32
BF16 = jnp.bfloat16

CHUNK = 64
N_META = 16
EPS = 1e-6
A_HD = 64
ROT_DIM = A_HD // 4
ROPE_THETA = 500000.0
S_HD = 64
S_GROUPS = 4
S_N = 128
S_CONV = 4
G_DK = 256
G_DV = 512
G_RANK = 16
GATE_TAU = 16.0
F_CONV = 3

LANE = 128
SUBLANE = 8
BF16_ROWS = 16
MXU_WIDTH = 256
V7X_VMEM_BYTES = 64 * 1024 * 1024
VMEM_CAP = V7X_VMEM_BYTES - 8 * 1024 * 1024

MM_VMEM_BUDGET = 44 * 1024 * 1024

FF_TILE = 1024
FF_SUB = 256
ATTN_QBLOCK = 512
SSD_CHUNK = 128
GLA_CHUNK = 32
GLA_BLOCK = 512


def _params(semantics, vmem_estimate):
    limit = int(min(VMEM_CAP, max(32 * 1024 * 1024, vmem_estimate * 3 // 2)))
    return pltpu.CompilerParams(dimension_semantics=semantics, vmem_limit_bytes=limit)


def _divisor_tile(n, limit, mult):
    best = None
    for t in range(mult, min(n, limit) + 1, mult):
        if n % t == 0:
            best = t
    assert best is not None, (n, limit, mult)
    return best


def _rms(x, g):
    return x * lax.rsqrt(jnp.mean(x * x, axis=-1, keepdims=True) + EPS) * g


def _split3(x):
    hi = x.astype(BF16)
    r1 = x - hi.astype(F32)
    mid = r1.astype(BF16)
    lo = (r1 - mid.astype(F32)).astype(BF16)
    return hi, mid, lo


def _dot(a, b):
    return jnp.dot(a, b, preferred_element_type=F32)


def _dot_nt(a, b):
    return lax.dot_general(a, b, (((1,), (1,)), ((), ())), preferred_element_type=F32)


def _dot_tn(a, b):
    return lax.dot_general(a, b, (((0,), (0,)), ((), ())), preferred_element_type=F32)


def _prenorm_kernel(h_ref, g_ref, o_ref):
    o_ref[...] = _rms(h_ref[...], g_ref[...]).astype(o_ref.dtype)


def prenorm(h, g):
    rows, d = h.shape
    tr = _divisor_tile(rows, 256, BF16_ROWS)
    est = 2 * tr * d * (4 + 2)
    return pl.pallas_call(
        _prenorm_kernel,
        out_shape=jax.ShapeDtypeStruct((rows, d), BF16),
        grid=(rows // tr,),
        in_specs=[pl.BlockSpec((tr, d), lambda i: (i, 0)), pl.BlockSpec((1, d), lambda i: (0, 0))],
        out_specs=pl.BlockSpec((tr, d), lambda i: (i, 0)),
        compiler_params=_params(("parallel",), est),
        name="prenorm",
    )(h, g.reshape(1, d))


def _resnorm_kernel(h_ref, u_ref, gpost_ref, gpre_ref, hnew_ref, hn_ref):
    hnew = h_ref[...] + _rms(u_ref[...].astype(F32), gpost_ref[...])
    hnew_ref[...] = hnew
    hn_ref[...] = _rms(hnew, gpre_ref[...]).astype(hn_ref.dtype)


def _resnorm_last_kernel(h_ref, u_ref, gpost_ref, hnew_ref):
    hnew_ref[...] = h_ref[...] + _rms(u_ref[...].astype(F32), gpost_ref[...])


def residual_norm(h, u, g_post, g_pre):
    rows, d = h.shape
    tr = _divisor_tile(rows, 256, BF16_ROWS)
    row_spec = pl.BlockSpec((tr, d), lambda i: (i, 0))
    g_spec = pl.BlockSpec((1, d), lambda i: (0, 0))
    if g_pre is None:
        return pl.pallas_call(
            _resnorm_last_kernel,
            out_shape=jax.ShapeDtypeStruct((rows, d), F32),
            grid=(rows // tr,),
            in_specs=[row_spec, row_spec, g_spec],
            out_specs=row_spec,
            compiler_params=_params(("parallel",), 2 * tr * d * 12),
            name="resnorm_last",
        )(h, u, g_post.reshape(1, d)), None
    return pl.pallas_call(
        _resnorm_kernel,
        out_shape=(jax.ShapeDtypeStruct((rows, d), F32), jax.ShapeDtypeStruct((rows, d), BF16)),
        grid=(rows // tr,),
        in_specs=[row_spec, row_spec, g_spec, g_spec],
        out_specs=(row_spec, row_spec),
        compiler_params=_params(("parallel",), 2 * tr * d * 14),
        name="resnorm",
    )(h, u, g_post.reshape(1, d), g_pre.reshape(1, d))


def _mm_kernel(*refs, nx):
    o_ref = refs[2 * nx]
    acc = _dot(refs[0][...], refs[nx][...])
    for t in range(1, nx):
        acc = acc + _dot(refs[t][...], refs[nx + t][...])
    o_ref[...] = acc.astype(o_ref.dtype)


def _mm_tiles(m, k, n, out_bytes):
    def search(w_bufs):
        best = None
        col_step = MXU_WIDTH if n % MXU_WIDTH == 0 else LANE
        for bn in range(col_step, min(n, 4096) + 1, col_step):
            if n % bn:
                continue
            for bm in range(BF16_ROWS, min(m, 2048) + 1, BF16_ROWS):
                if m % bm:
                    continue
                vm = 2 * bm * k * 2 + w_bufs * k * bn * 2 + 2 * bm * bn * out_bytes + bm * bn * 4
                if vm > MM_VMEM_BUDGET:
                    continue
                score = bm * bn / (bm + bn)
                if best is None or score > best[0]:
                    best = (score, bm, bn, vm, w_bufs)
        return best

    single, double = search(1), search(2)
    assert single is not None, (m, k, n)
    best = double if double is not None and double[0] >= 0.9 * single[0] else single
    return best[1:]


def matmul(xs, w, layer, n_cols=None, out_dtype=F32):
    m, kp = xs[0].shape
    assert all(x.shape == (m, kp) for x in xs)
    n = w.shape[2] if n_cols is None else n_cols
    bm, bn, vm, w_bufs = _mm_tiles(m, kp * len(xs), n, jnp.dtype(out_dtype).itemsize)

    def w_spec(t):
        return pl.BlockSpec((None, kp, bn), lambda j, i: (layer, t, j), pipeline_mode=pl.Buffered(w_bufs))

    return pl.pallas_call(
        functools.partial(_mm_kernel, nx=len(xs)),
        out_shape=jax.ShapeDtypeStruct((m, n), out_dtype),
        grid=(n // bn, m // bm),
        in_specs=[pl.BlockSpec((bm, kp), lambda j, i: (i, 0)) for _ in xs] + [w_spec(t) for t in range(len(xs))],
        out_specs=pl.BlockSpec((bm, bn), lambda j, i: (i, j)),
        compiler_params=_params(("parallel", "parallel"), vm),
        name="matmul",
    )(*xs, *([w] * len(xs)))


def _cast_pad_cols_kernel(x_ref, o_ref):
    valid = x_ref.shape[1]
    o_ref[:, :valid] = x_ref[...].astype(o_ref.dtype)
    if o_ref.shape[1] > valid:
        o_ref[:, valid:] = jnp.zeros((o_ref.shape[0], o_ref.shape[1] - valid), o_ref.dtype)


def cast_pad_halves(w, fp):
    layers, d, f2 = w.shape
    f = f2 // 2
    assert f % LANE == 0 and fp % LANE == 0
    tr = _divisor_tile(d, 128, BF16_ROWS)
    return pl.pallas_call(
        _cast_pad_cols_kernel,
        out_shape=jax.ShapeDtypeStruct((layers, d, 2 * fp), BF16),
        grid=(layers, 2, d // tr),
        in_specs=[pl.BlockSpec((None, tr, f), lambda l, s, r: (l, r, s))],
        out_specs=pl.BlockSpec((None, tr, fp), lambda l, s, r: (l, r, s)),
        compiler_params=_params(("parallel", "parallel", "parallel"), 2 * tr * (f * 4 + fp * 2)),
        name="cast_pad_halves",
    )(w)


def _cast_pad_rows_kernel(x_ref, o_ref, *, n_valid):
    r = pl.program_id(1)

    @pl.when(r < n_valid)
    def _():
        o_ref[...] = x_ref[...].astype(o_ref.dtype)

    @pl.when(r >= n_valid)
    def _():
        o_ref[...] = jnp.zeros(o_ref.shape, o_ref.dtype)


def cast_pad_rows(w, fp):
    layers, f, d = w.shape
    tr = _divisor_tile(math.gcd(f, fp), 256, BF16_ROWS)
    n_valid = f // tr
    return pl.pallas_call(
        functools.partial(_cast_pad_rows_kernel, n_valid=n_valid),
        out_shape=jax.ShapeDtypeStruct((layers, fp, d), BF16),
        grid=(layers, fp // tr),
        in_specs=[pl.BlockSpec((None, tr, d), lambda l, r: (l, jnp.minimum(r, n_valid - 1), 0))],
        out_specs=pl.BlockSpec((None, tr, d), lambda l, r: (l, r, 0)),
        compiler_params=_params(("parallel", "parallel"), 2 * tr * d * 6),
        name="cast_pad_rows",
    )(w)


def _ffn_up_kernel(x_ref, wg_ref, wv_ref, cw_ref, cb_ref, prev_ref, a_ref, st_ref, carry_ref,
                   *, nseq, rows, nblk):
    x = x_ref[...]
    tf = wg_ref.shape[1]
    if nblk > 1:
        first = (pl.program_id(1) % nblk) == 0

        @pl.when(first)
        def _():
            carry_ref[...] = jnp.zeros(carry_ref.shape, F32)

    t = lax.broadcasted_iota(jnp.int32, (1, rows, 1), 1)
    for c0 in range(0, tf, FF_SUB):
        cs = slice(c0, c0 + FF_SUB)
        g = _dot(x, wg_ref[:, cs])
        v = _dot(x, wv_ref[:, cs])
        if nblk == 1:
            prev = prev_ref[:, :, cs]
        else:
            prev = jnp.where(first, prev_ref[:, :, cs], carry_ref[SUBLANE - 2:SUBLANE, cs][None])
            carry_ref[:, cs] = g[rows - SUBLANE:rows, :]
        g3 = g.reshape(nseq, rows, FF_SUB)
        g1 = pltpu.roll(g, 1, axis=0).reshape(nseq, rows, FF_SUB)
        g2 = pltpu.roll(g, 2, axis=0).reshape(nseq, rows, FF_SUB)
        p0 = prev[:, 0:1, :]
        p1 = prev[:, 1:2, :]
        g1 = jnp.where(t == 0, p1, g1)
        g2 = jnp.where(t == 0, p0, jnp.where(t == 1, p1, g2))
        cw = cw_ref[:, cs]
        c = cw[0:1, :] * g2 + cw[1:2, :] * g1 + cw[2:3, :] * g3 + cb_ref[:, cs]
        a = jax.nn.gelu(c) * v.reshape(nseq, rows, FF_SUB)
        a_ref[:, cs] = a.reshape(nseq * rows, FF_SUB).astype(a_ref.dtype)
        st_ref[:, :, cs] = g3[:, rows - 2:rows, :]


def ffn_up(hn, w_up, conv_w, conv_b, layer, prev, seq_len):
    m, d = hn.shape
    f = w_up.shape[2] // 2
    nb = m // seq_len
    tf = FF_TILE
    nt = f // tf
    if seq_len >= 512:
        rows = _divisor_tile(seq_len, 1024, BF16_ROWS)
        nseq, nblk = 1, seq_len // rows
    else:
        rows, nblk = seq_len, 1
        nseq = _divisor_tile(nb, max(1, 1024 // seq_len), 1)
    bm = nseq * rows
    est = 2 * bm * d * 2 + 2 * d * tf * 2 + 2 * bm * tf * 2 + 10 * bm * FF_SUB * 4
    resident = dict(pipeline_mode=pl.Buffered(1))
    a, st = pl.pallas_call(
        functools.partial(_ffn_up_kernel, nseq=nseq, rows=rows, nblk=nblk),
        out_shape=(jax.ShapeDtypeStruct((m, f), BF16), jax.ShapeDtypeStruct((nb, F_CONV - 1, f), F32)),
        grid=(nt, m // bm),
        in_specs=[pl.BlockSpec((bm, d), lambda j, i: (i, 0)),
                  pl.BlockSpec((None, d, tf), lambda j, i: (layer, 0, j), **resident),
                  pl.BlockSpec((None, d, tf), lambda j, i: (layer, 0, nt + j), **resident),
                  pl.BlockSpec((None, F_CONV, tf), lambda j, i: (layer, 0, j)),
                  pl.BlockSpec((None, 1, tf), lambda j, i: (layer, 0, j)),
                  pl.BlockSpec((nseq, F_CONV - 1, tf), lambda j, i: (i // nblk, 0, j))],
        out_specs=(pl.BlockSpec((bm, tf), lambda j, i: (i, j)),
                   pl.BlockSpec((nseq, F_CONV - 1, tf), lambda j, i: (i // nblk, 0, j))),
        scratch_shapes=[pltpu.VMEM((SUBLANE, tf), F32)],
        compiler_params=_params(("parallel", "arbitrary"), est),
        name="ffn_up",
    )(hn, w_up, w_up, conv_w, conv_b, prev)
    return a, st


def _rope_tables(pos):
    half = ROT_DIM // 2
    inv = jnp.exp(-math.log(ROPE_THETA) * jnp.arange(half, dtype=F32) * 2.0 / ROT_DIM)
    ang = pos.astype(F32)[:, None] * inv[None, :]
    cos, sin = jnp.cos(ang), jnp.sin(ang)
    n = pos.shape[0]
    one = jnp.ones((n, A_HD - ROT_DIM), F32)
    zero = jnp.zeros((n, A_HD - ROT_DIM), F32)
    zh = jnp.zeros((n, half), F32)
    c = jnp.concatenate([cos, cos, one], axis=1)
    s_up = jnp.concatenate([-sin, zh, zero], axis=1)
    s_dn = jnp.concatenate([zh, sin, zero], axis=1)
    return tuple(jnp.concatenate([t, t], axis=1) for t in (c, s_up, s_dn))


def _rope(x, c, s_up, s_dn):
    half = ROT_DIM // 2
    return (x * c + pltpu.roll(x, 2 * A_HD - half, axis=1) * s_up + pltpu.roll(x, half, axis=1) * s_dn)


def _lambda(lq1, lk1, lq2, lk2, lambda_init):
    l1 = jnp.sum(lq1[...] * lk1[...], axis=-1, keepdims=True)
    l2 = jnp.sum(lq2[...] * lk2[...], axis=-1, keepdims=True)
    return jnp.exp(l1) - jnp.exp(l2) + lambda_init


def _two_softmax_pv(q, parts, lam):
    lane = lax.broadcasted_iota(jnp.int32, q.shape, 1)
    outs = []
    for sel in (lane < A_HD, lane >= A_HD):
        qs = jnp.where(sel, q, 0.0).astype(BF16)
        scores = []
        for kb, _, mask in parts:
            s = _dot_nt(qs, kb)
            if mask is not None:
                s = jnp.where(mask, s, -jnp.inf)
            scores.append(s)
        mx = scores[0].max(axis=-1, keepdims=True)
        for s in scores[1:]:
            mx = jnp.maximum(mx, s.max(axis=-1, keepdims=True))
        den = None
        num = None
        for s, (_, vb, _) in zip(scores, parts):
            p = jnp.exp(s - mx)
            d = jnp.sum(p, axis=-1, keepdims=True)
            o = _dot(p.astype(BF16), vb)
            den = d if den is None else den + d
            num = o if num is None else num + o
        outs.append(num / den)
    return outs[0] - lam * outs[1]


def _attn_prompt_kernel(q_ref, k_ref, v_ref, c_ref, su_ref, sd_ref, lq1, lk1, lq2, lk2, sub_ref,
                        o_ref, kr_ref, kb_ref, vb_ref, *, seq_len, pad_len, lambda_init):
    lam = _lambda(lq1, lk1, lq2, lk2, lambda_init)
    c, su, sd = c_ref[...], su_ref[...], sd_ref[...]
    k = _rope(k_ref[...], c, su, sd)
    kr_ref[...] = k
    kb_ref[0:seq_len, :] = k.astype(BF16)
    vb_ref[0:seq_len, :] = v_ref[...].astype(BF16)
    if pad_len > seq_len:
        kb_ref[seq_len:pad_len, :] = jnp.zeros((pad_len - seq_len, 2 * A_HD), BF16)
        vb_ref[seq_len:pad_len, :] = jnp.zeros((pad_len - seq_len, 2 * A_HD), BF16)
    qb = ATTN_QBLOCK
    shift = CHUNK - N_META
    for r0 in range(0, seq_len, qb):
        r1 = min(seq_len, r0 + qb)
        last_chunk = (r1 - 1 + shift) // CHUNK
        ke = min(pad_len, -(-(N_META + CHUNK * last_chunk) // LANE) * LANE)
        q = _rope(q_ref[r0:r1, :], c[r0:r1], su[r0:r1], sd[r0:r1]) * (A_HD ** -0.5)
        cq = (lax.broadcasted_iota(jnp.int32, (r1 - r0, 1), 0) + (r0 + shift)) // CHUNK
        ck = (lax.broadcasted_iota(jnp.int32, (1, ke), 1) + shift) // CHUNK
        o = _two_softmax_pv(q, [(kb_ref[0:ke, :], vb_ref[0:ke, :], cq >= ck)], lam)
        o_ref[r0:r1, :] = (_rms(o, sub_ref[...]) * (1.0 - lambda_init)).astype(o_ref.dtype)


def _attn_sample_kernel(q_ref, k_ref, v_ref, kp_ref, vp_ref, c_ref, su_ref, sd_ref, lq1, lk1, lq2, lk2,
                        sub_ref, o_ref, kr_ref, *, lambda_init):
    lam = _lambda(lq1, lk1, lq2, lk2, lambda_init)
    c, su, sd = c_ref[...], su_ref[...], sd_ref[...]
    k = _rope(k_ref[...], c, su, sd)
    kr_ref[...] = k
    q = _rope(q_ref[...], c, su, sd) * (A_HD ** -0.5)
    parts = [(kp_ref[0].astype(BF16), vp_ref[0].astype(BF16), None),
             (k.astype(BF16), v_ref[...].astype(BF16), None)]
    o = _two_softmax_pv(q, parts, lam)
    o_ref[...] = (_rms(o, sub_ref[...]) * (1.0 - lambda_init)).astype(o_ref.dtype)


def diff_attention(proj, pos, seq_len, n_heads, lams, subln, lambda_init, k_past=None, v_past=None, layer=0):
    m = proj.shape[0]
    nb = m // seq_len
    hw = 2 * A_HD
    tabs = _rope_tables(pos)
    vec = lambda: pl.BlockSpec((1, A_HD), lambda b, h: (0, 0))
    tab = lambda: pl.BlockSpec((seq_len, hw), lambda b, h: (0, 0))
    col = lambda off: pl.BlockSpec((seq_len, hw), lambda b, h: (b, off + h))
    out_specs = (pl.BlockSpec((seq_len, hw), lambda b, h: (b, h)),
                 pl.BlockSpec((seq_len, hw), lambda b, h: (b, h)))
    out_shape = (jax.ShapeDtypeStruct((m, n_heads * hw), BF16), jax.ShapeDtypeStruct((m, n_heads * hw), F32))
    lam_args = [l.reshape(1, A_HD) for l in lams]
    sub = subln.reshape(1, hw)
    sub_spec = pl.BlockSpec((1, hw), lambda b, h: (0, 0))
    if k_past is None:
        pad_len = -(-seq_len // LANE) * LANE
        est = 2 * seq_len * hw * (3 * 4 + 3 * 4 + 2 + 4) + 2 * pad_len * hw * 2 + 8 * LANE * pad_len * 4
        return pl.pallas_call(
            functools.partial(_attn_prompt_kernel, seq_len=seq_len, pad_len=pad_len, lambda_init=lambda_init),
            out_shape=out_shape,
            grid=(nb, n_heads),
            in_specs=[col(0), col(n_heads), col(2 * n_heads), tab(), tab(), tab(),
                      vec(), vec(), vec(), vec(), sub_spec],
            out_specs=out_specs,
            scratch_shapes=[pltpu.VMEM((pad_len, hw), BF16), pltpu.VMEM((pad_len, hw), BF16)],
            compiler_params=_params(("parallel", "parallel"), est),
            name="attn_prompt",
        )(proj, proj, proj, *tabs, *lam_args, sub)
    past = k_past.shape[2]
    kp, vp = k_past, v_past
    past_spec = lambda: pl.BlockSpec((None, 1, past, hw), lambda b, h: (layer, b, 0, h))
    est = 2 * (2 * past * hw * 4 + 8 * seq_len * hw * 4) + 8 * seq_len * past * 4
    return pl.pallas_call(
        functools.partial(_attn_sample_kernel, lambda_init=lambda_init),
        out_shape=out_shape,
        grid=(nb, n_heads),
        in_specs=[col(0), col(n_heads), col(2 * n_heads), past_spec(), past_spec(), tab(), tab(), tab(),
                  vec(), vec(), vec(), vec(), sub_spec],
        out_specs=out_specs,
        compiler_params=_params(("parallel", "parallel"), est),
        name="attn_sample",
    )(proj, proj, proj, kp, vp, *tabs, *lam_args, sub)


def _cumsum_rows(x):
    n = x.shape[0]
    row = lax.broadcasted_iota(jnp.int32, (n, 1), 0)
    sh = 1
    while sh < n:
        x = x + jnp.where(row >= sh, pltpu.roll(x, sh, axis=0), 0.0)
        sh *= 2
    return x


def _conv4_silu(ext, w, b, rows):
    acc = b + w[S_CONV - 1:S_CONV, :] * ext[SUBLANE:SUBLANE + rows, :]
    for j in range(S_CONV - 1):
        back = S_CONV - 1 - j
        acc = acc + w[j:j + 1, :] * pltpu.roll(ext, back, axis=0)[SUBLANE:SUBLANE + rows, :]
    return jax.nn.silu(acc)


def _ssd_kernel(xs_ref, bm_ref, cm_ref, z_ref, dt_ref, cwx_ref, cwb_ref, cwc_ref, cbx_ref, cbb_ref, cbc_ref,
                c0x_ref, c0b_ref, c0c_ref, dtb_ref, alog_ref, dsk_ref, nrm_ref, st0_ref,
                y_ref, stout_ref, s_ref, *, seq_len, chunk, has_state):
    hp = xs_ref.shape[1]
    nh = hp // S_HD
    if has_state:
        s_ref[...] = st0_ref[0].reshape(hp, S_N).T
    else:
        s_ref[...] = jnp.zeros((S_N, hp), F32)
    a_neg = -jnp.exp(alog_ref[...])
    dtb = dtb_ref[...]
    e_row = lax.broadcasted_iota(jnp.int32, (LANE, hp), 0)
    e_col = lax.broadcasted_iota(jnp.int32, (LANE, hp), 1) // S_HD
    expand = (e_row == e_col).astype(BF16)
    eye = (lax.broadcasted_iota(jnp.int32, (LANE, LANE), 0)
           == lax.broadcasted_iota(jnp.int32, (LANE, LANE), 1)).astype(BF16)
    lane_hp = lax.broadcasted_iota(jnp.int32, (1, 2 * S_HD), 1)

    def pad8(c0_ref):
        c0 = c0_ref[0]
        return jnp.concatenate([jnp.zeros((SUBLANE - (S_CONV - 1), c0.shape[1]), F32), c0], axis=0)

    def ext_of(ref, start, rows, c0_ref):
        if isinstance(start, int) and start == 0:
            return jnp.concatenate([pad8(c0_ref), ref[0:rows, :]], axis=0)
        return ref[pl.ds(start - SUBLANE, rows + SUBLANE), :]

    def to_heads(v):
        hi, mid, lo = _split3(v)
        return _dot(hi, expand) + _dot(mid, expand) + _dot(lo, expand)

    def step(start, rows):
        x = _conv4_silu(ext_of(xs_ref, start, rows, c0x_ref), cwx_ref[...], cbx_ref[...], rows)
        bmat = _conv4_silu(ext_of(bm_ref, start, rows, c0b_ref), cwb_ref[...], cbb_ref[...], rows)
        cmat = _conv4_silu(ext_of(cm_ref, start, rows, c0c_ref), cwc_ref[...], cbc_ref[...], rows)
        dt = jax.nn.softplus(dt_ref[pl.ds(start, rows), :] + dtb)
        cum = _cumsum_rows(dt * a_neg)
        hi, mid, lo = _split3(cum)
        cum_t = _dot_nt(eye, hi) + _dot_nt(eye, mid) + _dot_nt(eye, lo)
        cumx = to_heads(cum)
        xdt = x * to_heads(dt)
        xdt_b = xdt.astype(BF16)
        c_b = cmat.astype(BF16)
        b_b = bmat.astype(BF16)
        cb = _dot_nt(c_b, b_b)
        causal = (lax.broadcasted_iota(jnp.int32, (rows, rows), 0)
                  >= lax.broadcasted_iota(jnp.int32, (rows, rows), 1))
        cols = []
        for pair in range(nh // 2):
            xpair = xdt_b[:, pair * 2 * S_HD:(pair + 1) * 2 * S_HD]
            acc = None
            for sub in range(2):
                h = 2 * pair + sub
                seg = cum[:, h:h + 1] - cum_t[h:h + 1, :]
                wgt = (cb * jnp.exp(jnp.where(causal, seg, -jnp.inf))).astype(BF16)
                keep = (lane_hp < S_HD) if sub == 0 else (lane_hp >= S_HD)
                part = _dot(wgt, jnp.where(keep, xpair, jnp.zeros_like(xpair)))
                acc = part if acc is None else acc + part
            cols.append(acc)
        y = jnp.concatenate(cols, axis=1)
        s_old = s_ref[...]
        y = y + _dot(c_b, s_old.astype(BF16)) * jnp.exp(cumx)
        y = y + x * dsk_ref[...]
        last = cumx[rows - 1:rows, :]
        tail = jnp.exp(last - cumx)
        s_ref[...] = s_old * jnp.exp(last) + _dot_tn(b_b, (xdt * tail).astype(BF16))
        yz = y * jax.nn.silu(z_ref[pl.ds(start, rows), :])
        y_ref[pl.ds(start, rows), :] = _rms(yz, nrm_ref[...]).astype(y_ref.dtype)

    head = seq_len % chunk
    if head:
        step(0, head)
    n_full = seq_len // chunk
    if n_full == 1 and head == 0:
        step(0, chunk)
    elif n_full:
        def body(i, carry):
            step(pl.multiple_of(head + i * chunk, BF16_ROWS), chunk)
            return carry
        if head == 0:
            step(0, chunk)
            lax.fori_loop(1, n_full, body, 0)
        else:
            lax.fori_loop(0, n_full, body, 0)
    stout_ref[0] = s_ref[...].T.reshape(nh, S_HD, S_N)


def ssd_mixer(proj, dtp, x_off, z_off, seq_len, conv_w, conv_b, conv0, dt_bias, a_log, d_skip, s_norm, state0):
    m = proj.shape[0]
    nb = m // seq_len
    heads = a_log.shape[0]
    inner = heads * S_HD
    g = (conv_w.shape[1] - inner) // (2 * S_N)
    nh = heads // g
    hp = nh * S_HD
    chunk = min(SSD_CHUNK, seq_len)

    def per_head(v):
        return jnp.zeros((g, LANE), F32).at[:, :nh].set(v.reshape(g, nh)).reshape(1, g * LANE)

    xb, bb, cb = x_off // hp, (x_off + inner) // S_N, (x_off + inner + g * S_N) // S_N
    row = lambda width, blk: pl.BlockSpec((seq_len, width), lambda b, j: (b, blk + j))
    wrow = lambda rows, width, blk: pl.BlockSpec((rows, width), lambda b, j: (0, blk + j))
    c0 = lambda width, blk: pl.BlockSpec((1, S_CONV - 1, width), lambda b, j: (b, 0, blk + j))
    has_state = state0 is not None
    if not has_state:
        state0 = jnp.zeros((1, heads, S_HD, S_N), F32)
        st_spec = pl.BlockSpec((1, nh, S_HD, S_N), lambda b, j: (0, j, 0, 0))
    else:
        st_spec = pl.BlockSpec((1, nh, S_HD, S_N), lambda b, j: (b, j, 0, 0))
    cbias = conv_b.reshape(1, -1)
    ib, icb, icc = 0, inner // S_N, (inner + g * S_N) // S_N
    est = 2 * seq_len * (2 * hp * 4 + 3 * LANE * 4 + hp * 2) + 40 * chunk * hp * 4
    return pl.pallas_call(
        functools.partial(_ssd_kernel, seq_len=seq_len, chunk=chunk, has_state=has_state),
        out_shape=(jax.ShapeDtypeStruct((m, inner), BF16), jax.ShapeDtypeStruct((nb, heads, S_HD, S_N), F32)),
        grid=(nb, g),
        in_specs=[row(hp, xb), row(S_N, bb), row(S_N, cb), row(hp, z_off // hp),
                  pl.BlockSpec((seq_len, LANE), lambda b, j: (b, j)),
                  wrow(S_CONV, hp, ib), wrow(S_CONV, S_N, icb), wrow(S_CONV, S_N, icc),
                  wrow(1, hp, ib), wrow(1, S_N, icb), wrow(1, S_N, icc),
                  c0(hp, ib), c0(S_N, icb), c0(S_N, icc),
                  wrow(1, LANE, 0), wrow(1, LANE, 0), wrow(1, hp, 0), wrow(1, hp, 0), st_spec],
        out_specs=(pl.BlockSpec((seq_len, hp), lambda b, j: (b, j)),
                   pl.BlockSpec((1, nh, S_HD, S_N), lambda b, j: (b, j, 0, 0))),
        scratch_shapes=[pltpu.VMEM((S_N, hp), F32)],
        compiler_params=_params(("parallel", "parallel"), est),
        name="ssd",
    )(proj, proj, proj, proj, dtp, conv_w, conv_w, conv_w, cbias, cbias, cbias, conv0, conv0, conv0,
      per_head(dt_bias), per_head(a_log), jnp.repeat(d_skip, S_HD).reshape(1, inner), s_norm.reshape(1, inner),
      state0)


def _gla_kernel(q_ref, k_ref, v_ref, r_ref, gl_ref, w2_ref, bg_ref, gn_ref, s0_ref, o_ref, sout_ref,
                s_ref, cum_ref, qi_ref, kt_ref, of_ref, *, seq_len, chunk, has_state):
    if has_state:
        s_ref[...] = s0_ref[0, 0].T
    else:
        s_ref[...] = jnp.zeros(s_ref.shape, F32)
    scale = G_DK ** -0.5
    n_full = seq_len // chunk
    tail = seq_len - n_full * chunk
    assert tail & (tail - 1) == 0 and chunk & (chunk - 1) == 0

    def prep(start, rows, c):
        sl = pl.ds(start, rows)
        zg = _dot(gl_ref[sl, :].astype(BF16), w2_ref[...]) + bg_ref[...]
        cum = jax.nn.log_sigmoid(zg) / GATE_TAU
        pos = lax.broadcasted_iota(jnp.int32, (rows, 1), 0) & (c - 1)
        sh = 1
        while sh < c:
            cum = cum + jnp.where(pos >= sh, pltpu.roll(cum, sh, axis=0), 0.0)
            sh *= 2
        cum_ref[sl, :] = cum
        n = rows // c
        cum3 = cum.reshape(n, c, G_DK)
        anchor = cum3[:, c // 2 - 1:c // 2, :]
        last = cum3[:, c - 1:c, :]
        q3 = (q_ref[sl, :] * scale).reshape(n, c, G_DK)
        k3 = k_ref[sl, :].reshape(n, c, G_DK)
        flat = lambda a: a.reshape(rows, G_DK).astype(BF16)
        qi_ref[sl, :] = flat(q3 * jnp.exp(cum3))
        kt_ref[sl, :] = flat(k3 * jnp.exp(last - cum3))
        qa = (q3 * jnp.exp(cum3 - anchor)).astype(BF16)
        ka = (k3 * jnp.exp(anchor - cum3)).astype(BF16)
        att = jnp.einsum("ctd,csd->cts", qa, ka, preferred_element_type=F32)
        causal = (lax.broadcasted_iota(jnp.int32, (1, c, c), 1) >= lax.broadcasted_iota(jnp.int32, (1, c, c), 2))
        att = jnp.where(causal, att, 0.0).astype(BF16)
        v3 = v_ref[sl, :].astype(BF16).reshape(n, c, G_DV)
        of_ref[sl, :] = jnp.einsum("cts,cse->cte", att, v3, preferred_element_type=F32).reshape(rows, G_DV)

    def post(start, rows):
        sl = pl.ds(start, rows)
        o_ref[sl, :] = (_rms(of_ref[sl, :], gn_ref[...]) * jax.nn.silu(r_ref[sl, :])).astype(o_ref.dtype)

    def blocks(fn, *extra):
        main = n_full * chunk
        nblk = main // GLA_BLOCK
        if nblk:
            def body(i, carry):
                fn(pl.multiple_of(i * GLA_BLOCK, GLA_BLOCK), GLA_BLOCK, *extra)
                return carry
            lax.fori_loop(0, nblk, body, 0)
        if main > nblk * GLA_BLOCK:
            fn(nblk * GLA_BLOCK, main - nblk * GLA_BLOCK, *extra)

    blocks(prep, chunk)
    if tail:
        prep(n_full * chunk, tail, tail)

    def step(start, c):
        sl = pl.ds(start, c)
        s_old = s_ref[...]
        of_ref[sl, :] += _dot_nt(qi_ref[sl, :], s_old.astype(BF16))
        last = cum_ref[pl.ds(start + c - SUBLANE, SUBLANE), :][SUBLANE - 1:SUBLANE, :]
        s_ref[...] = s_old * jnp.exp(last) + _dot_tn(v_ref[sl, :].astype(BF16), kt_ref[sl, :])

    def body(i, carry):
        step(pl.multiple_of(i * chunk, chunk), chunk)
        return carry
    lax.fori_loop(0, n_full, body, 0, unroll=4 if n_full % 4 == 0 else 1)
    if tail:
        step(n_full * chunk, tail)

    blocks(post)
    if tail:
        post(n_full * chunk, tail)
    sout_ref[0, 0] = s_ref[...].T


def gla_mixer(proj, glow, seq_len, n_heads, w_gate2, b_gate, g_norm, state0):
    m = proj.shape[0]
    nb = m // seq_len
    kq, vq = n_heads * G_DK, n_heads * G_DV
    has_state = state0 is not None
    if not has_state:
        state0 = jnp.zeros((1, 1, G_DK, G_DV), F32)
        st_spec = pl.BlockSpec((1, 1, G_DK, G_DV), lambda b, h: (0, 0, 0, 0))
    else:
        st_spec = pl.BlockSpec((1, 1, G_DK, G_DV), lambda b, h: (b, h, 0, 0))
    w2 = jnp.zeros((LANE, kq), BF16).at[:G_RANK].set(w_gate2.astype(BF16))
    row = lambda width, blk: pl.BlockSpec((seq_len, width), lambda b, h: (b, blk + h))
    est = (2 * seq_len * (2 * G_DK * 4 + 2 * G_DV * 4 + LANE * 4 + G_DV * 2)
           + seq_len * (G_DK * 4 + 4 * G_DK * 2 + G_DV * 4) + 8 * G_DK * G_DV * 4)
    return pl.pallas_call(
        functools.partial(_gla_kernel, seq_len=seq_len, chunk=GLA_CHUNK, has_state=has_state),
        out_shape=(jax.ShapeDtypeStruct((m, vq), BF16), jax.ShapeDtypeStruct((nb, n_heads, G_DK, G_DV), F32)),
        grid=(nb, n_heads),
        in_specs=[row(G_DK, 0), row(G_DK, kq // G_DK), row(G_DV, 2 * kq // G_DV), row(G_DV, (2 * kq + vq) // G_DV),
                  pl.BlockSpec((seq_len, LANE), lambda b, h: (b, 0)),
                  pl.BlockSpec((LANE, G_DK), lambda b, h: (0, h)),
                  pl.BlockSpec((1, G_DK), lambda b, h: (0, h)),
                  pl.BlockSpec((1, G_DV), lambda b, h: (0, 0)),
                  st_spec],
        out_specs=(pl.BlockSpec((seq_len, G_DV), lambda b, h: (b, h)),
                   pl.BlockSpec((1, 1, G_DK, G_DV), lambda b, h: (b, h, 0, 0))),
        scratch_shapes=[pltpu.VMEM((G_DV, G_DK), F32), pltpu.VMEM((seq_len, G_DK), F32),
                        pltpu.VMEM((seq_len, G_DK), BF16), pltpu.VMEM((seq_len, G_DK), BF16),
                        pltpu.VMEM((seq_len, G_DV), F32)],
        compiler_params=_params(("parallel", "parallel"), est),
        name="gla",
    )(proj, proj, proj, proj, glow, w2, b_gate.reshape(1, kq), g_norm.reshape(1, G_DV), state0)


def _prepare_weights(p, d_model):
    depth = p["w_up"].shape[0]
    n_even = p["w_in_even"].shape[0]
    heads_s = p["ssm_a_log"].shape[1]
    inner = heads_s * S_HD
    a_width = d_model // 2
    a_qk = a_width
    conv_dim = p["ssm_conv_w"].shape[2]
    groups = (conv_dim - inner) // (2 * S_N)
    main_e = 2 * a_qk + a_width + inner + conv_dim
    nh = heads_s // groups
    f = p["w_up"].shape[2] // 2
    fp = -(-f // FF_TILE) * FF_TILE
    kq = p["gla_w_gate2"].shape[2]
    main_o = 2 * kq + 2 * d_model
    w = {}
    w["in_even"] = p["w_in_even"].astype(BF16)
    dtw = p["w_in_even"][:, :, main_e:].reshape(n_even, d_model, groups, nh)
    w["dt"] = jnp.pad(dtw, ((0, 0), (0, 0), (0, 0), (0, LANE - nh))).reshape(n_even, d_model, groups * LANE).astype(BF16)
    w["out_even"] = p["w_out_even"].astype(BF16)
    w["in_odd"] = p["w_in_odd"].astype(BF16)
    w["glow"] = jnp.pad(p["w_in_odd"][:, :, main_o:], ((0, 0), (0, 0), (0, LANE - G_RANK))).astype(BF16)
    w["out_odd"] = p["w_out_odd"].astype(BF16)
    w["up"] = cast_pad_halves(p["w_up"], fp)
    w["down"] = cast_pad_rows(p["w_down"], fp)
    w["ffn_conv_w"] = jnp.pad(p["ffn_conv_w"], ((0, 0), (0, 0), (0, fp - f)))
    w["ffn_conv_b"] = jnp.pad(p["ffn_conv_b"], ((0, 0), (0, fp - f))).reshape(depth, 1, fp)
    w["f"], w["fp"], w["main_e"], w["main_o"] = f, fp, main_e, main_o
    return w


def _run_trunk(h, seq_len, pos, k_past, v_past, ssm0, sconv0, gla0, fconv0, p, w):
    rows, d = h.shape
    nb = rows // seq_len
    depth = p["w_up"].shape[0]
    a_heads = d // 2 // (2 * A_HD)
    a_w = a_heads * 2 * A_HD
    heads_s = p["ssm_a_log"].shape[1]
    inner = heads_s * S_HD
    conv_dim = p["ssm_conv_w"].shape[2]
    g_heads = d // G_DV
    f, fp = w["f"], w["fp"]
    ks, vs, ssms, sconvs, glas, fconvs = [], [], [], [], [], []
    hn = prenorm(h, p["norm_pre_mix"][0])
    for i in range(depth):
        if i % 2 == 0:
            e = i // 2
            lambda_init = 0.8 - 0.6 * math.exp(-0.3 * i)
            proj = matmul([hn], w["in_even"], e, n_cols=w["main_e"])
            dtp = matmul([hn], w["dt"], e)
            lams = (p["lambda_q1"][e], p["lambda_k1"][e], p["lambda_q2"][e], p["lambda_k2"][e])
            o_a, k_rot = diff_attention(proj, pos, seq_len, a_heads, lams, p["attn_subln"][e], lambda_init,
                                        k_past, v_past, e)
            conv0 = jnp.zeros((nb, S_CONV - 1, conv_dim), F32) if sconv0 is None else sconv0[e]
            y_s, ssm_new = ssd_mixer(proj, dtp, 3 * a_w + inner, 3 * a_w, seq_len, p["ssm_conv_w"][e],
                                     p["ssm_conv_b"][e], conv0, p["ssm_dt_bias"][e], p["ssm_a_log"][e],
                                     p["ssm_d"][e], p["ssm_norm"][e], None if ssm0 is None else ssm0[e])
            out = matmul([o_a, y_s], w["out_even"], e, out_dtype=BF16)
            proj3 = proj.reshape(nb, seq_len, -1)
            ks.append(k_rot.reshape(nb, seq_len, 2 * a_heads, A_HD))
            vs.append(proj3[:, :, 2 * a_w:3 * a_w].reshape(nb, seq_len, a_heads, 2 * A_HD))
            ssms.append(ssm_new)
            sconvs.append(proj3[:, seq_len - (S_CONV - 1):, 3 * a_w + inner:3 * a_w + inner + conv_dim])
        else:
            o = i // 2
            proj = matmul([hn], w["in_odd"], o, n_cols=w["main_o"])
            glow = matmul([hn], w["glow"], o)
            y_g, gla_new = gla_mixer(proj, glow, seq_len, g_heads, p["gla_w_gate2"][o], p["gla_b_gate"][o],
                                     p["gla_norm"][o], None if gla0 is None else gla0[o])
            out = matmul([y_g], w["out_odd"], o, out_dtype=BF16)
            glas.append(gla_new)
        h, hn = residual_norm(h, out, p["norm_post_mix"][i], p["norm_pre_ffn"][i])
        prev = jnp.zeros((nb, F_CONV - 1, fp), F32) if fconv0 is None else jnp.pad(fconv0[i], ((0, 0), (0, 0), (0, fp - f)))
        a, fconv_new = ffn_up(hn, w["up"], w["ffn_conv_w"], w["ffn_conv_b"], i, prev, seq_len)
        fconvs.append(fconv_new[:, :, :f])
        ff = matmul([a], w["down"], i, out_dtype=BF16)
        h, hn = residual_norm(h, ff, p["norm_post_ffn"][i], p["norm_pre_mix"][i + 1] if i + 1 < depth else None)
    return (h, jnp.stack(ks), jnp.stack(vs), jnp.stack(ssms), jnp.stack(sconvs), jnp.stack(glas), jnp.stack(fconvs))


def kernel(x_prompt, x_sample, cache_k, cache_v, state_ssm, state_ssm_conv, state_gla, state_ffn_conv, meta_tokens, norm_pre_mix, norm_post_mix, norm_pre_ffn, norm_post_ffn, w_in_even, lambda_q1, lambda_k1, lambda_q2, lambda_k2, attn_subln, ssm_conv_w, ssm_conv_b, ssm_dt_bias, ssm_a_log, ssm_d, ssm_norm, w_out_even, w_in_odd, gla_w_gate2, gla_b_gate, gla_norm, w_out_odd, w_up, ffn_conv_w, ffn_conv_b, w_down):
    p = dict(norm_pre_mix=norm_pre_mix, norm_post_mix=norm_post_mix, norm_pre_ffn=norm_pre_ffn,
             norm_post_ffn=norm_post_ffn, w_in_even=w_in_even, lambda_q1=lambda_q1, lambda_k1=lambda_k1,
             lambda_q2=lambda_q2, lambda_k2=lambda_k2, attn_subln=attn_subln, ssm_conv_w=ssm_conv_w,
             ssm_conv_b=ssm_conv_b, ssm_dt_bias=ssm_dt_bias, ssm_a_log=ssm_a_log, ssm_d=ssm_d,
             ssm_norm=ssm_norm, w_out_even=w_out_even, w_in_odd=w_in_odd, gla_w_gate2=gla_w_gate2,
             gla_b_gate=gla_b_gate, gla_norm=gla_norm, w_out_odd=w_out_odd, w_up=w_up,
             ffn_conv_w=ffn_conv_w, ffn_conv_b=ffn_conv_b, w_down=w_down)
    b, seq, d = x_prompt.shape
    w = _prepare_weights(p, d)
    n_meta = meta_tokens.shape[0]
    lp = n_meta + seq
    h0 = jnp.concatenate([jnp.broadcast_to(meta_tokens[None], (b, n_meta, d)), x_prompt], axis=1).reshape(b * lp, d)
    hp, p_k, p_v, p_ssm, p_ssm_conv, p_gla, p_ffn_conv = _run_trunk(
        h0, lp, jnp.arange(lp), None, None, None, None, None, None, p, w)
    y_prompt = hp.reshape(b, lp, d)[:, n_meta:]
    bs, ls, _ = x_sample.shape
    pos_s = cache_k.shape[2] + jnp.arange(ls)
    cache_k = cache_k.reshape(cache_k.shape[:3] + (-1,))
    cache_v = cache_v.reshape(cache_v.shape[:3] + (-1,))
    ys, s_k, s_v, s_ssm, s_ssm_conv, s_gla, s_ffn_conv = _run_trunk(
        x_sample.reshape(bs * ls, d), ls, pos_s, cache_k, cache_v, state_ssm, state_ssm_conv, state_gla,
        state_ffn_conv, p, w)
    return (y_prompt, ys.reshape(bs, ls, d), p_k, p_v, p_ssm, p_ssm_conv, p_gla, p_ffn_conv,
            s_k, s_v, s_ssm, s_ssm_conv, s_gla, s_ffn_conv)
```
